```python
import math
import jax, jax.numpy as jnp
from jax import lax
import numpy as np

D_MODEL = 1024
BATCH = 16
SEQ = 2048
DEPTH = 4
DEC_BATCH = 128
DEC_SEQ = 1
PAST_LEN = 8192
PAGE_SIZE = 128

SSM_EXPAND = 2
D_INNER = SSM_EXPAND * D_MODEL
SSM_HEADDIM = 64
SSM_HEADS = D_INNER // SSM_HEADDIM
SSM_GROUPS = 8
HEADS_PER_GROUP = SSM_HEADS // SSM_GROUPS
D_STATE = 128
CONV_W = 4
CONV_DIM = D_INNER + 2 * SSM_GROUPS * D_STATE
SSD_CHUNK = 128
MLA_HEADS = 16
QK_NOPE = 128
QK_ROPE = 64
V_DIM = 128
Q_LORA = 256
KV_LORA = 256
ROPE_BASE = 10000.0
Q_BLOCK = 128
IN_DIM = D_INNER + CONV_DIM + SSM_HEADS + Q_LORA + KV_LORA + QK_ROPE + 2 * D_MODEL
D_FF = 2816
N_EXPERTS = 8
TOP_K = 2
D_FF_EXPERT = 2816
N_DENSE = (DEPTH + 1) // 2
N_MOE = DEPTH // 2
ALPHA = (2.0 * DEPTH) ** 0.25
BETA = (8.0 * DEPTH) ** -0.25
LN_EPS = 1e-5
RMS_EPS = 1e-6

kernel_name = "hybrid_ssd_mla_deepnorm_moe_step"


def _rmsnorm(x, g):
    xf = x.astype(jnp.float32)
    y = xf * lax.rsqrt(jnp.mean(xf * xf, axis=-1, keepdims=True) + RMS_EPS)
    return (y * g.astype(jnp.float32)).astype(x.dtype)


def _layernorm(x, g, b):
    xf = x.astype(jnp.float32)
    mu = jnp.mean(xf, axis=-1, keepdims=True)
    var = jnp.mean(jnp.square(xf - mu), axis=-1, keepdims=True)
    y = (xf - mu) * lax.rsqrt(var + LN_EPS)
    return (y * g.astype(jnp.float32) + b.astype(jnp.float32)).astype(x.dtype)


def _rope_tables(pos):
    inv = ROPE_BASE ** (-jnp.arange(0, QK_ROPE, 2, dtype=jnp.float32) / QK_ROPE)
    ang = pos.astype(jnp.float32)[:, None] * inv[None, :]
    return jnp.cos(ang), jnp.sin(ang)


def _apply_rope(x, cos, sin):
    xf = x.astype(jnp.float32)
    x1, x2 = xf[..., : QK_ROPE // 2], xf[..., QK_ROPE // 2:]
    return jnp.concatenate([x1 * cos - x2 * sin, x1 * sin + x2 * cos], axis=-1).astype(x.dtype)


def _causal_conv(u, prev, w, b):
    t = u.shape[1]
    up = jnp.concatenate([prev.astype(u.dtype), u], axis=1)
    out = b + up[:, 0:t] * w[0]
    for k in range(1, CONV_W):
        out = out + up[:, k:k + t] * w[k]
    return jax.nn.silu(out), up[:, t:]


def _ssd(xh, dt, a, bm, cm, h0):
    f32 = jnp.float32
    b_, t = xh.shape[:2]
    lc = SSD_CHUNK if t % SSD_CHUNK == 0 else t
    nc = t // lc
    da = dt.astype(f32) * a.astype(f32)
    xdt = xh.astype(f32) * dt.astype(f32)[..., None]

    def chunks(v):
        return v.reshape((b_, nc, lc) + v.shape[2:])

    xc, dac, bc, cc = chunks(xdt), chunks(da), chunks(bm.astype(f32)), chunks(cm.astype(f32))
    acum = jnp.cumsum(dac, axis=2)
    diff = acum[:, :, :, None] - acum[:, :, None, :]
    causal = jnp.tril(jnp.ones((lc, lc), dtype=bool))[None, None, :, :, None, None]
    decay = jnp.exp(jnp.where(causal, diff, -jnp.inf))
    cb = jnp.einsum("bclgn,bcsgn->bclsg", cc, bc)
    y_diag = jnp.einsum("bclsg,bclsgr,bcsgrp->bclgrp", cb, decay, xc)
    to_end = jnp.exp(acum[:, :, -1:] - acum)
    chunk_states = jnp.einsum("bclgn,bclgr,bclgrp->bcgrpn", bc, to_end, xc)
    chunk_decay = jnp.exp(acum[:, :, -1])

    def step(h, inp):
        s, d = inp
        return d[..., None, None] * h + s, h

    h_final, h_in = lax.scan(step, h0.astype(f32),
                             (jnp.moveaxis(chunk_states, 1, 0), jnp.moveaxis(chunk_decay, 1, 0)))
    h_in = jnp.moveaxis(h_in, 0, 1)
    y_off = jnp.einsum("bclgn,bcgrpn,bclgr->bclgrp", cc, h_in, jnp.exp(acum))
    y = (y_diag + y_off).reshape(xh.shape)
    return y.astype(xh.dtype), h_final.astype(h0.dtype)


def _ssm_branch(z, xbc, dt_raw, prev_conv, h0, conv_w, conv_b, dt_bias, a_log, d_skip, norm_g):
    b_, t = z.shape[:2]
    xbc, conv_new = _causal_conv(xbc, prev_conv, conv_w, conv_b)
    gn = SSM_GROUPS * D_STATE
    xs = xbc[..., :D_INNER].reshape(b_, t, SSM_GROUPS, HEADS_PER_GROUP, SSM_HEADDIM)
    bm = xbc[..., D_INNER:D_INNER + gn].reshape(b_, t, SSM_GROUPS, D_STATE)
    cm = xbc[..., D_INNER + gn:].reshape(b_, t, SSM_GROUPS, D_STATE)
    dt = jax.nn.softplus((dt_raw + dt_bias).astype(jnp.float32)).reshape(b_, t, SSM_GROUPS, HEADS_PER_GROUP)
    a = -jnp.exp(a_log.astype(jnp.float32)).reshape(SSM_GROUPS, HEADS_PER_GROUP)
    h0g = h0.reshape(b_, SSM_GROUPS, HEADS_PER_GROUP, SSM_HEADDIM, D_STATE)
    y, h_new = _ssd(xs, dt, a, bm, cm, h0g)
    y = y + xs * d_skip.reshape(SSM_GROUPS, HEADS_PER_GROUP)[:, :, None]
    y = _rmsnorm(y.reshape(b_, t, D_INNER) * jax.nn.silu(z), norm_g)
    return y, h_new.reshape(b_, SSM_HEADS, SSM_HEADDIM, D_STATE), conv_new


def _latent_attention(q_lat, q_rope, q_pos, keys_c, keys_kr, key_pos):
    b_, t = q_lat.shape[:2]
    nb = t // Q_BLOCK if t % Q_BLOCK == 0 else 1
    blk = t // nb
    scale = (QK_NOPE + QK_ROPE) ** -0.5

    def blocks(v):
        return jnp.moveaxis(v.reshape((b_, nb, blk) + v.shape[2:]), 1, 0)

    def one(args):
        ql, qr, qp = args
        s = (jnp.einsum("bqhr,bkr->bhqk", ql, keys_c)
             + jnp.einsum("bqhe,bke->bhqk", qr, keys_kr)).astype(jnp.float32) * scale
        s = jnp.where((key_pos[None, :] <= qp[:, None])[None, None], s, -jnp.inf)
        p = jax.nn.softmax(s, axis=-1).astype(keys_c.dtype)
        return jnp.einsum("bhqk,bkr->bqhr", p, keys_c)

    out = lax.map(one, (blocks(q_lat), blocks(q_rope), q_pos.reshape(nb, blk)))
    return jnp.moveaxis(out, 0, 1).reshape(b_, t, MLA_HEADS, KV_LORA)


def _mla_branch(q_a, kv_a, pos, past_c, past_kr, past_pos, q_norm_g, w_qb, kv_norm_g, w_kvb):
    b_, t = q_a.shape[:2]
    q = (_rmsnorm(q_a, q_norm_g) @ w_qb).reshape(b_, t, MLA_HEADS, QK_NOPE + QK_ROPE)
    cos, sin = _rope_tables(pos)
    q_nope = q[..., :QK_NOPE]
    q_rope = _apply_rope(q[..., QK_NOPE:], cos[:, None, :], sin[:, None, :])
    c_new = _rmsnorm(kv_a[..., :KV_LORA], kv_norm_g)
    kr_new = _apply_rope(kv_a[..., KV_LORA:], cos, sin)
    wkv = w_kvb.reshape(KV_LORA, MLA_HEADS, QK_NOPE + V_DIM)
    q_lat = jnp.einsum("bthd,rhd->bthr", q_nope, wkv[..., :QK_NOPE])
    if past_c is None:
        keys_c, keys_kr, key_pos = c_new, kr_new, pos
    else:
        keys_c = jnp.concatenate([past_c.astype(c_new.dtype), c_new], axis=1)
        keys_kr = jnp.concatenate([past_kr.astype(kr_new.dtype), kr_new], axis=1)
        key_pos = jnp.concatenate([past_pos, pos])
    o_lat = _latent_attention(q_lat, q_rope, pos, keys_c, keys_kr, key_pos)
    o = jnp.einsum("bthr,rhv->bthv", o_lat, wkv[..., QK_NOPE:]).reshape(b_, t, MLA_HEADS * V_DIM)
    return o, c_new, kr_new


def _mixer(x, pos, past_c, past_kr, past_pos, h0, prev_conv, w_in, conv_w, conv_b, dt_bias, a_log,
           d_skip, ssm_norm_g, q_norm_g, w_qb, kv_norm_g, w_kvb, w_br_ssm, w_br_mla, w_out):
    proj = x @ w_in
    cuts = np.cumsum([D_INNER, CONV_DIM, SSM_HEADS, Q_LORA, KV_LORA + QK_ROPE, D_MODEL]).tolist()
    z, xbc, dt_raw, q_a, kv_a, g_ssm, g_mla = jnp.split(proj, cuts, axis=-1)
    y_ssm, h_new, conv_new = _ssm_branch(z, xbc, dt_raw, prev_conv, h0, conv_w, conv_b,
                                         dt_bias, a_log, d_skip, ssm_norm_g)
    y_mla, c_new, kr_new = _mla_branch(q_a, kv_a, pos, past_c, past_kr, past_pos,
                                       q_norm_g, w_qb, kv_norm_g, w_kvb)
    merged = jax.nn.sigmoid(g_ssm) * (y_ssm @ w_br_ssm) + jax.nn.sigmoid(g_mla) * (y_mla @ w_br_mla)
    return merged @ w_out, c_new, kr_new, h_new, conv_new


def _swiglu(x, wg, wu, wd):
    return (jax.nn.silu(x @ wg) * (x @ wu)) @ wd


def _moe(x, router_w, wg, wu, wd):
    logits = (x @ router_w).astype(jnp.float32)
    top_v, top_i = lax.top_k(logits, TOP_K)
    gates = jax.nn.softmax(top_v, axis=-1)
    combine = jnp.sum(jax.nn.one_hot(top_i, N_EXPERTS, dtype=jnp.float32) * gates[..., None], axis=-2)
    out = jnp.zeros_like(x)
    for e in range(N_EXPERTS):
        out = out + combine[..., e:e + 1].astype(x.dtype) * _swiglu(x, wg[e], wu[e], wd[e])
    return out


def setup_inputs(seed: int = 0) -> dict:
    key = jax.random.key(seed)
    ks = iter(jax.random.split(key, 48))
    f32 = jnp.float32

    def nrm(shape, scale):
        return jax.random.normal(next(ks), shape, f32) * scale

    def gain(shape):
        return 1.0 + nrm(shape, 0.02)

    n_pages = PAST_LEN // PAGE_SIZE
    n_pool = (DEC_BATCH * n_pages * 5) // 4
    perm = jax.random.permutation(next(ks), n_pool)
    page_table = perm[: DEC_BATCH * n_pages].reshape(DEC_BATCH, n_pages).astype(jnp.int32)
    dt0 = jnp.exp(jax.random.uniform(next(ks), (DEPTH, SSM_HEADS), f32, math.log(1e-3), math.log(1e-1)))
    dt_bias = dt0 + jnp.log(-jnp.expm1(-dt0))
    a_log = jnp.log(jax.random.uniform(next(ks), (DEPTH, SSM_HEADS), f32, 1.0, 16.0))
    return {
        "x_prompt": nrm((BATCH, SEQ, D_MODEL), 1.0),
        "x_sample": nrm((DEC_BATCH, DEC_SEQ, D_MODEL), 1.0),
        "cache_latent": nrm((DEPTH, n_pool, PAGE_SIZE, KV_LORA), 1.0),
        "cache_krope": nrm((DEPTH, n_pool, PAGE_SIZE, QK_ROPE), 1.0),
        "state_ssm": nrm((DEPTH, DEC_BATCH, SSM_HEADS, SSM_HEADDIM, D_STATE), 0.5),
        "state_conv": nrm((DEPTH, DEC_BATCH, CONV_W - 1, CONV_DIM), 1.0),
        "page_table": page_table,
        "w_in": nrm((DEPTH, D_MODEL, IN_DIM), D_MODEL ** -0.5),
        "conv_w": nrm((DEPTH, CONV_W, CONV_DIM), CONV_W ** -0.5),
        "conv_b": nrm((DEPTH, CONV_DIM), 0.02),
        "dt_bias": dt_bias,
        "a_log": a_log,
        "d_skip": gain((DEPTH, SSM_HEADS)),
        "ssm_norm_g": gain((DEPTH, D_INNER)),
        "q_norm_g": gain((DEPTH, Q_LORA)),
        "w_qb": nrm((DEPTH, Q_LORA, MLA_HEADS * (QK_NOPE + QK_ROPE)), Q_LORA ** -0.5),
        "kv_norm_g": gain((DEPTH, KV_LORA)),
        "w_kvb": nrm((DEPTH, KV_LORA, MLA_HEADS * (QK_NOPE + V_DIM)), KV_LORA ** -0.5),
        "w_br_ssm": nrm((DEPTH, D_INNER, D_MODEL), BETA * D_INNER ** -0.5),
        "w_br_mla": nrm((DEPTH, MLA_HEADS * V_DIM, D_MODEL), BETA * (MLA_HEADS * V_DIM) ** -0.5),
        "w_out": nrm((DEPTH, D_MODEL, D_MODEL), BETA * D_MODEL ** -0.5),
        "ln1_g": gain((DEPTH, D_MODEL)),
        "ln1_b": nrm((DEPTH, D_MODEL), 0.02),
        "ln2_g": gain((DEPTH, D_MODEL)),
        "ln2_b": nrm((DEPTH, D_MODEL), 0.02),
        "ffn_w_gate": nrm((N_DENSE, D_MODEL, D_FF), D_MODEL ** -0.5),
        "ffn_w_up": nrm((N_DENSE, D_MODEL, D_FF), D_MODEL ** -0.5),
        "ffn_w_down": nrm((N_DENSE, D_FF, D_MODEL), BETA * D_FF ** -0.5),
        "router_w": nrm((N_MOE, D_MODEL, N_EXPERTS), D_MODEL ** -0.5),
        "moe_w_gate": nrm((N_MOE, N_EXPERTS, D_MODEL, D_FF_EXPERT), D_MODEL ** -0.5),
        "moe_w_up": nrm((N_MOE, N_EXPERTS, D_MODEL, D_FF_EXPERT), D_MODEL ** -0.5),
        "moe_w_down": nrm((N_MOE, N_EXPERTS, D_FF_EXPERT, D_MODEL), BETA * D_FF_EXPERT ** -0.5),
    }


def reference(x_prompt, x_sample, cache_latent, cache_krope, state_ssm, state_conv, page_table,
              w_in, conv_w, conv_b, dt_bias, a_log, d_skip, ssm_norm_g, q_norm_g, w_qb, kv_norm_g,
              w_kvb, w_br_ssm, w_br_mla, w_out, ln1_g, ln1_b, ln2_g, ln2_b, ffn_w_gate, ffn_w_up,
              ffn_w_down, router_w, moe_w_gate, moe_w_up, moe_w_down):
    n_prompt, t_prompt = x_prompt.shape[:2]
    n_seq, t_sample = x_sample.shape[:2]
    n_past = page_table.shape[1] * PAGE_SIZE
    pos_p = jnp.arange(t_prompt, dtype=jnp.int32)
    pos_s = PAST_LEN + jnp.arange(t_sample, dtype=jnp.int32)
    past_pos = jnp.arange(n_past, dtype=jnp.int32)
    h0_p = jnp.zeros((n_prompt, SSM_HEADS, SSM_HEADDIM, D_STATE), x_prompt.dtype)
    conv0_p = jnp.zeros((n_prompt, CONV_W - 1, CONV_DIM), x_prompt.dtype)

    xp, xs = x_prompt, x_sample
    lat_p, kr_p, ssm_p, conv_p = [], [], [], []
    lat_s, kr_s, ssm_s, conv_s = [], [], [], []
    for l in range(DEPTH):
        lw = (w_in[l], conv_w[l], conv_b[l], dt_bias[l], a_log[l], d_skip[l], ssm_norm_g[l],
              q_norm_g[l], w_qb[l], kv_norm_g[l], w_kvb[l], w_br_ssm[l], w_br_mla[l], w_out[l])
        mp, c_p, k_p, h_p, cv_p = _mixer(xp, pos_p, None, None, None, h0_p, conv0_p, *lw)
        past_c = cache_latent[l][page_table].reshape(n_seq, n_past, KV_LORA)
        past_kr = cache_krope[l][page_table].reshape(n_seq, n_past, QK_ROPE)
        ms, c_s, k_s, h_s, cv_s = _mixer(xs, pos_s, past_c, past_kr, past_pos,
                                         state_ssm[l], state_conv[l], *lw)
        lat_p.append(c_p); kr_p.append(k_p); ssm_p.append(h_p); conv_p.append(cv_p)
        lat_s.append(c_s); kr_s.append(k_s); ssm_s.append(h_s); conv_s.append(cv_s)
        xp = _layernorm(ALPHA * xp + mp, ln1_g[l], ln1_b[l])
        xs = _layernorm(ALPHA * xs + ms, ln1_g[l], ln1_b[l])
        if l % 2 == 0:
            i = l // 2
            fp = _swiglu(xp, ffn_w_gate[i], ffn_w_up[i], ffn_w_down[i])
            fs = _swiglu(xs, ffn_w_gate[i], ffn_w_up[i], ffn_w_down[i])
        else:
            i = l // 2
            fp = _moe(xp, router_w[i], moe_w_gate[i], moe_w_up[i], moe_w_down[i])
            fs = _moe(xs, router_w[i], moe_w_gate[i], moe_w_up[i], moe_w_down[i])
        xp = _layernorm(ALPHA * xp + fp, ln2_g[l], ln2_b[l])
        xs = _layernorm(ALPHA * xs + fs, ln2_g[l], ln2_b[l])

    return (xp, xs,
            jnp.stack(lat_p), jnp.stack(kr_p), jnp.stack(ssm_p), jnp.stack(conv_p),
            jnp.stack(lat_s), jnp.stack(kr_s), jnp.stack(ssm_s), jnp.stack(conv_s))
```

```python
import functools
import math

import jax
import jax.numpy as jnp
from jax import lax
from jax.experimental import pallas as pl
from jax.experimental.pallas import tpu as pltpu

F32 = jnp.float32
BF16 = jnp.bfloat16

D_MODEL = 1024
DEPTH = 4
PAGE_SIZE = 128
D_INNER = 2048
SSM_HEADDIM = 64
SSM_HEADS = 32
SSM_GROUPS = 8
HEADS_PER_GROUP = 4
D_STATE = 128
CONV_W = 4
CONV_DIM = 4096
SSD_CHUNK = 128
MLA_HEADS = 16
QK_NOPE = 128
QK_ROPE = 64
V_DIM = 128
Q_LORA = 256
KV_LORA = 256
ROPE_BASE = 10000.0
N_EXPERTS = 8
ALPHA = (2.0 * DEPTH) ** 0.25
LN_EPS = 1e-5
RMS_EPS = 1e-6

LANES = 128
SUBLANES = 8
VMEM_LIMIT = 56 * 1024 * 1024

SM_QA = 0
SM_LAT = Q_LORA
SM_TAIL = Q_LORA + KV_LORA
SM_W = SM_TAIL + LANES
DT_OFF = QK_ROPE
NEG = -1e30


def _dot(a, b):
    return jnp.dot(a, b, preferred_element_type=F32)


def _dot_nt(a, b):
    return lax.dot_general(a, b, (((1,), (1,)), ((), ())), preferred_element_type=F32)


def _dot_tn(a, b):
    return lax.dot_general(a, b, (((0,), (0,)), ((), ())), preferred_element_type=F32)


def _silu(x):
    return x * jax.nn.sigmoid(x)


def _softplus(x):
    return jnp.maximum(x, 0.0) + jnp.log1p(jnp.exp(-jnp.abs(x)))


def _params(*sem):
    return pltpu.CompilerParams(dimension_semantics=sem, vmem_limit_bytes=VMEM_LIMIT)


def _layernorm(r, g, b):
    mu = jnp.mean(r, axis=-1, keepdims=True)
    d = r - mu
    var = jnp.mean(d * d, axis=-1, keepdims=True)
    return d * lax.rsqrt(var + LN_EPS) * g + b


def _rms(x, g):
    return x * lax.rsqrt(jnp.mean(x * x, axis=-1, keepdims=True) + RMS_EPS) * g


def _mm_kernel(x_ref, w_ref, o_ref):
    o_ref[...] = _dot(x_ref[...].astype(BF16), w_ref[...]).astype(o_ref.dtype)


def _matmul(x, w, out_dtype, tm=1024, tn=1024):
    m, k = x.shape
    n = w.shape[1]
    tm = min(tm, m)
    tn = min(tn, n)
    return pl.pallas_call(
        _mm_kernel,
        grid=(m // tm, n // tn),
        in_specs=[pl.BlockSpec((tm, k), lambda i, j: (i, 0)),
                  pl.BlockSpec((k, tn), lambda i, j: (0, j))],
        out_specs=pl.BlockSpec((tm, tn), lambda i, j: (i, j)),
        out_shape=jax.ShapeDtypeStruct((m, n), out_dtype),
        compiler_params=_params("arbitrary", "arbitrary"),
        name="in_proj",
    )(x, w)


def _expand_heads(col_blk, lane, j):
    a = col_blk[:, DT_OFF + 2 * j:DT_OFF + 2 * j + 1]
    b = col_blk[:, DT_OFF + 2 * j + 1:DT_OFF + 2 * j + 2]
    return jnp.where(lane < SSM_HEADDIM, a, b)


def _ssd_kernel(xbc_ref, z_ref, sm_ref, cw_ref, cb_ref, dtb_ref, alog_ref, dexp_ref, ng_ref,
                y_ref, hout_ref, cout_ref, ext_ref, h_ref, ybuf_ref, *, L):
    c = pl.program_id(1)
    nc = pl.num_programs(1)
    rp = HEADS_PER_GROUP * SSM_HEADDIM
    gn = SSM_GROUPS * D_STATE
    hist = SUBLANES

    u = xbc_ref[0].astype(F32)

    @pl.when(c == 0)
    def _():
        ext_ref[0:hist, :] = jnp.zeros((hist, CONV_DIM), F32)
        h_ref[...] = jnp.zeros_like(h_ref)

    @pl.when(c > 0)
    def _():
        ext_ref[0:hist, :] = ext_ref[L:L + hist, :]

    ext_ref[hist:hist + L, :] = u
    acc = cb_ref[...] + u * cw_ref[CONV_W - 1:CONV_W, :]
    for s in range(1, CONV_W):
        acc = acc + ext_ref[hist - s:hist - s + L, :] * cw_ref[CONV_W - 1 - s:CONV_W - s, :]
    act = _silu(acc)

    tail = sm_ref[0, :, SM_TAIL:SM_TAIL + LANES]
    dt = _softplus(tail + dtb_ref[...])
    da = dt * (-jnp.exp(alog_ref[...]))
    row = lax.broadcasted_iota(jnp.int32, (L, L), 0)
    col = lax.broadcasted_iota(jnp.int32, (L, L), 1)
    causal = row >= col
    tri = causal.astype(F32)
    acum = jnp.dot(tri, da, precision=lax.Precision.HIGHEST, preferred_element_type=F32)
    acum_t = acum.T
    a_last = acum[L - 1:L, :]
    to_end = jnp.exp(a_last - acum)
    e_acum = jnp.exp(acum)
    cdec = jnp.exp(a_last)
    lane = lax.broadcasted_iota(jnp.int32, (L, LANES), 1)

    def expand_group(v, g):
        return jnp.concatenate([_expand_heads(v, lane, 2 * g), _expand_heads(v, lane, 2 * g + 1)], axis=1)

    for g in range(SSM_GROUPS):
        xs_g = act[:, g * rp:(g + 1) * rp]
        bb = act[:, D_INNER + g * D_STATE:D_INNER + (g + 1) * D_STATE].astype(BF16)
        cbf = act[:, D_INNER + gn + g * D_STATE:D_INNER + gn + (g + 1) * D_STATE].astype(BF16)
        xdt = xs_g * expand_group(dt, g)
        xdtb = xdt.astype(BF16)
        cb = _dot_nt(cbf, bb)
        yd = []
        for r in range(HEADS_PER_GROUP):
            hd = DT_OFF + g * HEADS_PER_GROUP + r
            seg = acum[:, hd:hd + 1] - acum_t[hd:hd + 1, :]
            dec = jnp.exp(jnp.where(causal, seg, NEG))
            yd.append(_dot((cb * dec).astype(BF16), xdtb[:, r * SSM_HEADDIM:(r + 1) * SSM_HEADDIM]))
        y_diag = jnp.concatenate(yd, axis=1)
        hg = h_ref[g * rp:(g + 1) * rp, :]
        y_off = _dot_nt(cbf, hg.astype(BF16)) * expand_group(e_acum, g)
        xw = (xdt * expand_group(to_end, g)).astype(BF16)
        st = _dot_tn(xw, bb)
        for r in range(HEADS_PER_GROUP):
            hd = DT_OFF + g * HEADS_PER_GROUP + r
            lo = g * rp + r * SSM_HEADDIM
            h_ref[lo:lo + SSM_HEADDIM, :] = (cdec[0:1, hd:hd + 1] * hg[r * SSM_HEADDIM:(r + 1) * SSM_HEADDIM, :]
                                             + st[r * SSM_HEADDIM:(r + 1) * SSM_HEADDIM, :])
        ybuf_ref[:, g * rp:(g + 1) * rp] = y_diag + y_off + xs_g * dexp_ref[:, g * rp:(g + 1) * rp]

    yz = ybuf_ref[...] * _silu(z_ref[0].astype(F32))
    y_ref[0] = _rms(yz, ng_ref[...]).astype(y_ref.dtype)

    @pl.when(c == nc - 1)
    def _():
        hout_ref[0] = h_ref[...]
        cout_ref[0] = ext_ref[hist + L - (CONV_W - 1):hist + L, :]


def _ssd_prompt(big, small, lw):
    nb, t, _ = big.shape
    L = SSD_CHUNK if t % SSD_CHUNK == 0 else t
    nc = t // L
    kern = functools.partial(_ssd_kernel, L=L)
    const = lambda b, c: (0, 0)
    return pl.pallas_call(
        kern,
        grid=(nb, nc),
        in_specs=[
            pl.BlockSpec((1, L, CONV_DIM), lambda b, c: (b, c, 0)),
            pl.BlockSpec((1, L, D_INNER), lambda b, c: (b, c, 2)),
            pl.BlockSpec((1, L, SM_W), lambda b, c: (b, c, 0)),
            pl.BlockSpec((CONV_W, CONV_DIM), const),
            pl.BlockSpec((1, CONV_DIM), const),
            pl.BlockSpec((1, LANES), const),
            pl.BlockSpec((1, LANES), const),
            pl.BlockSpec((1, D_INNER), const),
            pl.BlockSpec((1, D_INNER), const),
        ],
        out_specs=[
            pl.BlockSpec((1, L, D_INNER), lambda b, c: (b, c, 0)),
            pl.BlockSpec((1, D_INNER, D_STATE), lambda b, c: (b, 0, 0)),
            pl.BlockSpec((1, CONV_W - 1, CONV_DIM), lambda b, c: (b, 0, 0)),
        ],
        out_shape=[
            jax.ShapeDtypeStruct((nb, t, D_INNER), BF16),
            jax.ShapeDtypeStruct((nb, D_INNER, D_STATE), F32),
            jax.ShapeDtypeStruct((nb, CONV_W - 1, CONV_DIM), F32),
        ],
        scratch_shapes=[
            pltpu.VMEM((L + 2 * SUBLANES, CONV_DIM), F32),
            pltpu.VMEM((D_INNER, D_STATE), F32),
            pltpu.VMEM((L, D_INNER), F32),
        ],
        compiler_params=_params("arbitrary", "arbitrary"),
        name="ssd_prompt",
    )(big, big, small, lw["conv_w"], lw["conv_b"], lw["dt_bias"], lw["a_log"], lw["d_exp"], lw["ssm_norm_g"])


def _ssd_step_kernel(xbc_ref, z_ref, sm_ref, cs_ref, hs_ref, cw_ref, cb_ref, dtb_ref, alog_ref, dexp_ref, ng_ref,
                     y_ref, hout_ref, cout_ref):
    gn = SSM_GROUPS * D_STATE
    u = xbc_ref[0].astype(F32)
    prev = cs_ref[0]
    acc = cb_ref[...] + u * cw_ref[CONV_W - 1:CONV_W, :]
    for k in range(CONV_W - 1):
        acc = acc + prev[k:k + 1, :] * cw_ref[k:k + 1, :]
    cout_ref[0, 0:CONV_W - 2, :] = prev[1:CONV_W - 1, :]
    cout_ref[0, CONV_W - 2:CONV_W - 1, :] = u
    act = _silu(acc)

    tail = sm_ref[0, :, SM_TAIL:SM_TAIL + LANES]
    dt = _softplus(tail + dtb_ref[...])
    d_a = jnp.exp(dt * (-jnp.exp(alog_ref[...])))
    lane = lax.broadcasted_iota(jnp.int32, (1, LANES), 1)
    ys = []
    for j in range(D_INNER // LANES):
        g = (j * LANES) // (HEADS_PER_GROUP * SSM_HEADDIM)
        xs_j = act[:, j * LANES:(j + 1) * LANES]
        xdt = xs_j * _expand_heads(dt, lane, j)
        xcol = jnp.broadcast_to(xdt, (LANES, LANES)).T
        dcol = jnp.broadcast_to(_expand_heads(d_a, lane, j), (LANES, LANES)).T
        b_g = act[:, D_INNER + g * D_STATE:D_INNER + (g + 1) * D_STATE]
        c_g = act[:, D_INNER + gn + g * D_STATE:D_INNER + gn + (g + 1) * D_STATE]
        hnew = dcol * hs_ref[0, j * LANES:(j + 1) * LANES, :] + xcol * b_g
        hout_ref[0, j * LANES:(j + 1) * LANES, :] = hnew
        yrow = jnp.sum((hnew * c_g).T, axis=0, keepdims=True)
        ys.append(yrow + xs_j * dexp_ref[:, j * LANES:(j + 1) * LANES])
    y = jnp.concatenate(ys, axis=1)
    yz = y * _silu(z_ref[0].astype(F32))
    y_ref[0] = _rms(yz, ng_ref[...]).astype(y_ref.dtype)


def _ssd_step(big, small, conv_state, ssm_state, lw):
    n = big.shape[0]
    const = lambda s: (0, 0)
    return pl.pallas_call(
        _ssd_step_kernel,
        grid=(n,),
        in_specs=[
            pl.BlockSpec((1, 1, CONV_DIM), lambda s: (s, 0, 0)),
            pl.BlockSpec((1, 1, D_INNER), lambda s: (s, 0, 2)),
            pl.BlockSpec((1, 1, SM_W), lambda s: (s, 0, 0)),
            pl.BlockSpec((1, CONV_W - 1, CONV_DIM), lambda s: (s, 0, 0)),
            pl.BlockSpec((1, D_INNER, D_STATE), lambda s: (s, 0, 0)),
            pl.BlockSpec((CONV_W, CONV_DIM), const),
            pl.BlockSpec((1, CONV_DIM), const),
            pl.BlockSpec((1, LANES), const),
            pl.BlockSpec((1, LANES), const),
            pl.BlockSpec((1, D_INNER), const),
            pl.BlockSpec((1, D_INNER), const),
        ],
        out_specs=[
            pl.BlockSpec((1, 1, D_INNER), lambda s: (s, 0, 0)),
            pl.BlockSpec((1, D_INNER, D_STATE), lambda s: (s, 0, 0)),
            pl.BlockSpec((1, CONV_W - 1, CONV_DIM), lambda s: (s, 0, 0)),
        ],
        out_shape=[
            jax.ShapeDtypeStruct((n, 1, D_INNER), BF16),
            jax.ShapeDtypeStruct((n, D_INNER, D_STATE), F32),
            jax.ShapeDtypeStruct((n, CONV_W - 1, CONV_DIM), F32),
        ],
        compiler_params=_params("arbitrary"),
        name="ssd_step",
    )(big, big, small, conv_state, ssm_state, lw["conv_w"], lw["conv_b"], lw["dt_bias"], lw["a_log"],
      lw["d_exp"], lw["ssm_norm_g"])


def _prep_kernel(sm_ref, cos_ref, sin_ref, qg_ref, kvg_ref, wqn_ref, wqr_ref, wkt_ref,
                 qlat_ref, qrope_ref, kc_ref, kr_ref, latf_ref, krf_ref, *, scale):
    sm = sm_ref[0]
    tm = sm.shape[0]
    half = QK_ROPE // 2
    qn = _rms(sm[:, SM_QA:SM_QA + Q_LORA], qg_ref[...]).astype(BF16)
    qnope = _dot(qn, wqn_ref[...])
    qr = _dot(qn, wqr_ref[...])
    cos = cos_ref[...]
    sin = sin_ref[...]
    wq = MLA_HEADS * QK_ROPE
    reps = wq // LANES
    cosq = jnp.concatenate([cos] * reps, axis=1)
    sinq = jnp.concatenate([sin] * reps, axis=1)
    lane_q = lax.broadcasted_iota(jnp.int32, (tm, wq), 1)
    swap = jnp.where((lane_q & (QK_ROPE - 1)) < half, pltpu.roll(qr, wq - half, 1), pltpu.roll(qr, half, 1))
    qrot = (qr * cosq + swap * sinq) * scale
    for h in range(MLA_HEADS):
        qrope_ref[0, h] = qrot[:, h * QK_ROPE:(h + 1) * QK_ROPE].astype(BF16)
        ql = _dot(qnope[:, h * QK_NOPE:(h + 1) * QK_NOPE].astype(BF16), wkt_ref[h]) * scale
        qlat_ref[0, h] = ql.astype(BF16)
    cn = _rms(sm[:, SM_LAT:SM_LAT + KV_LORA], kvg_ref[...])
    latf_ref[0] = cn
    kc_ref[0] = cn.astype(BF16)
    tail = sm[:, SM_TAIL:SM_TAIL + LANES]
    lane_k = lax.broadcasted_iota(jnp.int32, (tm, LANES), 1)
    swapk = jnp.where(lane_k < half, pltpu.roll(tail, LANES - half, 1), pltpu.roll(tail, half, 1))
    krot = (tail * cos + swapk * sin)[:, 0:QK_ROPE]
    krf_ref[0] = krot
    kr_ref[0] = krot.astype(BF16)


def _mla_prep(small, cos, sin, lw, tm=256):
    nb, t, _ = small.shape
    tm = min(tm, t)
    scale = (QK_NOPE + QK_ROPE) ** -0.5
    const2 = lambda b, i: (0, 0)
    return pl.pallas_call(
        functools.partial(_prep_kernel, scale=scale),
        grid=(nb, t // tm),
        in_specs=[
            pl.BlockSpec((1, tm, SM_W), lambda b, i: (b, i, 0)),
            pl.BlockSpec((tm, LANES), lambda b, i: (i, 0)),
            pl.BlockSpec((tm, LANES), lambda b, i: (i, 0)),
            pl.BlockSpec((1, Q_LORA), const2),
            pl.BlockSpec((1, KV_LORA), const2),
            pl.BlockSpec((Q_LORA, MLA_HEADS * QK_NOPE), const2),
            pl.BlockSpec((Q_LORA, MLA_HEADS * QK_ROPE), const2),
            pl.BlockSpec((MLA_HEADS, QK_NOPE, KV_LORA), lambda b, i: (0, 0, 0)),
        ],
        out_specs=[
            pl.BlockSpec((1, MLA_HEADS, tm, KV_LORA), lambda b, i: (b, 0, i, 0)),
            pl.BlockSpec((1, MLA_HEADS, tm, QK_ROPE), lambda b, i: (b, 0, i, 0)),
            pl.BlockSpec((1, tm, KV_LORA), lambda b, i: (b, i, 0)),
            pl.BlockSpec((1, tm, QK_ROPE), lambda b, i: (b, i, 0)),
            pl.BlockSpec((1, tm, KV_LORA), lambda b, i: (b, i, 0)),
            pl.BlockSpec((1, tm, QK_ROPE), lambda b, i: (b, i, 0)),
        ],
        out_shape=[
            jax.ShapeDtypeStruct((nb, MLA_HEADS, t, KV_LORA), BF16),
            jax.ShapeDtypeStruct((nb, MLA_HEADS, t, QK_ROPE), BF16),
            jax.ShapeDtypeStruct((nb, t, KV_LORA), BF16),
            jax.ShapeDtypeStruct((nb, t, QK_ROPE), BF16),
            jax.ShapeDtypeStruct((nb, t, KV_LORA), F32),
            jax.ShapeDtypeStruct((nb, t, QK_ROPE), F32),
        ],
        compiler_params=_params("arbitrary", "arbitrary"),
        name="mla_prep",
    )(small, cos, sin, lw["q_norm_g"], lw["kv_norm_g"], lw["w_q_nope"], lw["w_q_rope"], lw["w_k_t"])


def _attn_kernel(qlat_ref, qrope_ref, kc_ref, kr_ref, wv_ref, o_ref, m_ref, l_ref, acc_ref, *, tq, tk):
    i = pl.program_id(1)
    rows = MLA_HEADS * tq
    q1 = qlat_ref[0].reshape(rows, KV_LORA)
    q2 = qrope_ref[0].reshape(rows, QK_ROPE)
    m_ref[...] = jnp.full(m_ref.shape, NEG, F32)
    l_ref[...] = jnp.zeros(l_ref.shape, F32)
    acc_ref[...] = jnp.zeros(acc_ref.shape, F32)

    def block(j, masked):
        start = pl.multiple_of(j * tk, tk)
        kc = kc_ref[0, pl.ds(start, tk), :]
        kr = kr_ref[0, pl.ds(start, tk), :]
        s = _dot_nt(q1, kc) + _dot_nt(q2, kr)
        if masked:
            qpos = i * tq + lax.broadcasted_iota(jnp.int32, (MLA_HEADS, tq, tk), 1).reshape(rows, tk)
            kpos = j * tk + lax.broadcasted_iota(jnp.int32, (rows, tk), 1)
            s = jnp.where(kpos <= qpos, s, NEG)
        m_old = m_ref[...]
        m_new = jnp.maximum(m_old, jnp.max(s, axis=1, keepdims=True))
        alpha = jnp.exp(m_old - m_new)
        p = jnp.exp(s - m_new)
        l_ref[...] = alpha * l_ref[...] + jnp.sum(p, axis=1, keepdims=True)
        acc_ref[...] = alpha * acc_ref[...] + _dot(p.astype(BF16), kc)
        m_ref[...] = m_new

    nfull = (i * tq) // tk

    def body(j, carry):
        block(j, False)
        return carry

    lax.fori_loop(0, nfull, body, 0)
    block(nfull, True)
    o_lat = (acc_ref[...] / l_ref[...]).astype(BF16)
    for h in range(MLA_HEADS):
        o_ref[0, :, h * V_DIM:(h + 1) * V_DIM] = _dot(o_lat[h * tq:(h + 1) * tq, :], wv_ref[h]).astype(o_ref.dtype)


def _attn_prompt(qlat, qrope, kc, kr, wv, tq=128, tk=256):
    nb, _, t, _ = qlat.shape
    tk = min(tk, t)
    tq = min(tq, tk)
    rows = MLA_HEADS * tq
    return pl.pallas_call(
        functools.partial(_attn_kernel, tq=tq, tk=tk),
        grid=(nb, t // tq),
        in_specs=[
            pl.BlockSpec((1, MLA_HEADS, tq, KV_LORA), lambda b, i: (b, 0, i, 0)),
            pl.BlockSpec((1, MLA_HEADS, tq, QK_ROPE), lambda b, i: (b, 0, i, 0)),
            pl.BlockSpec((1, t, KV_LORA), lambda b, i: (b, 0, 0)),
            pl.BlockSpec((1, t, QK_ROPE), lambda b, i: (b, 0, 0)),
            pl.BlockSpec((MLA_HEADS, KV_LORA, V_DIM), lambda b, i: (0, 0, 0)),
        ],
        out_specs=pl.BlockSpec((1, tq, MLA_HEADS * V_DIM), lambda b, i: (b, i, 0)),
        out_shape=jax.ShapeDtypeStruct((nb, t, MLA_HEADS * V_DIM), BF16),
        scratch_shapes=[pltpu.VMEM((rows, 1), F32), pltpu.VMEM((rows, 1), F32), pltpu.VMEM((rows, KV_LORA), F32)],
        compiler_params=_params("arbitrary", "arbitrary"),
        name="attn_prompt",
    )(qlat, qrope, kc, kr, wv)


def _decode_kernel(pt_ref, qlat_ref, qrope_ref, kcn_ref, krn_ref, *refs, n_pages):
    lat_refs = refs[:n_pages]
    kr_refs = refs[n_pages:2 * n_pages]
    o_ref, m_ref, l_ref, acc_ref = refs[2 * n_pages:]
    del pt_ref
    pc = pl.program_id(1)
    pad = jnp.zeros((LANES - MLA_HEADS, KV_LORA), BF16)
    q1 = jnp.concatenate([qlat_ref[0], pad], axis=0)
    q2 = jnp.concatenate([qrope_ref[0], pad[:, 0:QK_ROPE]], axis=0)

    @pl.when(pc == 0)
    def _():
        kn = jnp.broadcast_to(kcn_ref[0], (2 * SUBLANES, KV_LORA))
        rn = jnp.broadcast_to(krn_ref[0], (2 * SUBLANES, QK_ROPE))
        s_new = (_dot_nt(kn, q1) + _dot_nt(rn, q2))[0:1, :]
        m_ref[...] = s_new
        l_ref[...] = jnp.ones(l_ref.shape, F32)
        acc_ref[...] = jnp.broadcast_to(kcn_ref[0].astype(F32), acc_ref.shape)

    lat = jnp.concatenate([r[0, 0] for r in lat_refs], axis=0).astype(BF16)
    kr = jnp.concatenate([r[0, 0] for r in kr_refs], axis=0).astype(BF16)
    s_t = _dot_nt(lat, q1) + _dot_nt(kr, q2)
    m_old = m_ref[...]
    m_new = jnp.maximum(m_old, jnp.max(s_t, axis=0, keepdims=True))
    alpha = jnp.exp(m_old - m_new)
    p = jnp.exp(s_t - m_new)
    l_ref[...] = alpha * l_ref[...] + jnp.sum(p, axis=0, keepdims=True)
    m_ref[...] = m_new
    a_col = jnp.broadcast_to(alpha, (LANES, LANES)).T
    a_col = jnp.concatenate([a_col] * (KV_LORA // LANES), axis=1)
    acc_ref[...] = a_col * acc_ref[...] + _dot_tn(p.astype(BF16), lat)

    @pl.when(pc == pl.num_programs(1) - 1)
    def _():
        l_col = jnp.broadcast_to(l_ref[...], (LANES, LANES)).T
        l_col = jnp.concatenate([l_col] * (KV_LORA // LANES), axis=1)
        o_ref[0] = (acc_ref[...] / l_col)[0:MLA_HEADS, :].astype(o_ref.dtype)


def _attn_decode(layer, page_table, cache_latent, cache_krope, qlat, qrope, kcn, krn, pages_per_step=8):
    n, npages = page_table.shape
    n_pg = math.gcd(pages_per_step, npages)
    pt = page_table.reshape(-1)

    def page_map(k):
        return lambda s, pc, pt_ref: (layer, pt_ref[s * npages + pc * n_pg + k], 0, 0)

    row = lambda s, pc, pt_ref: (s, 0, 0)
    in_specs = [
        pl.BlockSpec((1, MLA_HEADS, KV_LORA), row),
        pl.BlockSpec((1, MLA_HEADS, QK_ROPE), row),
        pl.BlockSpec((1, 1, KV_LORA), row),
        pl.BlockSpec((1, 1, QK_ROPE), row),
    ]
    in_specs += [pl.BlockSpec((1, 1, PAGE_SIZE, KV_LORA), page_map(k)) for k in range(n_pg)]
    in_specs += [pl.BlockSpec((1, 1, PAGE_SIZE, QK_ROPE), page_map(k)) for k in range(n_pg)]
    grid_spec = pltpu.PrefetchScalarGridSpec(
        num_scalar_prefetch=1,
        grid=(n, npages // n_pg),
        in_specs=in_specs,
        out_specs=pl.BlockSpec((1, MLA_HEADS, KV_LORA), row),
        scratch_shapes=[pltpu.VMEM((1, LANES), F32), pltpu.VMEM((1, LANES), F32), pltpu.VMEM((LANES, KV_LORA), F32)],
    )
    return pl.pallas_call(
        functools.partial(_decode_kernel, n_pages=n_pg),
        grid_spec=grid_spec,
        out_shape=jax.ShapeDtypeStruct((n, MLA_HEADS, KV_LORA), BF16),
        compiler_params=_params("arbitrary", "arbitrary"),
        name="attn_decode",
    )(pt, qlat, qrope, kcn, krn, *([cache_latent] * n_pg), *([cache_krope] * n_pg))


def _vproj_kernel(o_ref, wv_ref, y_ref):
    for h in range(MLA_HEADS):
        y_ref[:, h * V_DIM:(h + 1) * V_DIM] = _dot(o_ref[h], wv_ref[h]).astype(y_ref.dtype)


def _vproj(o_lat, wv):
    n = o_lat.shape[1]
    return pl.pallas_call(
        _vproj_kernel,
        out_shape=jax.ShapeDtypeStruct((n, MLA_HEADS * V_DIM), BF16),
        name="v_proj",
    )(o_lat, wv)


def _merge_kernel(ys_ref, ym_ref, g_ref, x_ref, wbs_ref, wbm_ref, wo_ref, lg_ref, lb_ref, o_ref):
    gates = g_ref[...].astype(F32)
    a = _dot(ys_ref[...], wbs_ref[...])
    b = _dot(ym_ref[...], wbm_ref[...])
    m = jax.nn.sigmoid(gates[:, 0:D_MODEL]) * a + jax.nn.sigmoid(gates[:, D_MODEL:2 * D_MODEL]) * b
    r = ALPHA * x_ref[...] + _dot(m.astype(BF16), wo_ref[...])
    o_ref[...] = _layernorm(r, lg_ref[...], lb_ref[...])


def _merge(y_ssm, y_mla, big, x, lw, tm=512):
    m = x.shape[0]
    tm = min(tm, m)
    const = lambda i: (0, 0)
    return pl.pallas_call(
        _merge_kernel,
        grid=(m // tm,),
        in_specs=[
            pl.BlockSpec((tm, D_INNER), lambda i: (i, 0)),
            pl.BlockSpec((tm, MLA_HEADS * V_DIM), lambda i: (i, 0)),
            pl.BlockSpec((tm, 2 * D_MODEL), lambda i: (i, 3)),
            pl.BlockSpec((tm, D_MODEL), lambda i: (i, 0)),
            pl.BlockSpec((D_INNER, D_MODEL), const),
            pl.BlockSpec((MLA_HEADS * V_DIM, D_MODEL), const),
            pl.BlockSpec((D_MODEL, D_MODEL), const),
            pl.BlockSpec((1, D_MODEL), const),
            pl.BlockSpec((1, D_MODEL), const),
        ],
        out_specs=pl.BlockSpec((tm, D_MODEL), lambda i: (i, 0)),
        out_shape=jax.ShapeDtypeStruct((m, D_MODEL), F32),
        compiler_params=_params("arbitrary"),
        name="merge_ln",
    )(y_ssm, y_mla, big, x, lw["w_br_ssm"], lw["w_br_mla"], lw["w_out"], lw["ln1_g"], lw["ln1_b"])


def _ffn_kernel(x_ref, wg_ref, wu_ref, wd_ref, lg_ref, lb_ref, o_ref, acc_ref):
    j = pl.program_id(1)
    xb = x_ref[...].astype(BF16)
    h = (_silu(_dot(xb, wg_ref[...])) * _dot(xb, wu_ref[...])).astype(BF16)
    part = _dot(h, wd_ref[...])

    @pl.when(j == 0)
    def _():
        acc_ref[...] = part

    @pl.when(j > 0)
    def _():
        acc_ref[...] += part

    @pl.when(j == pl.num_programs(1) - 1)
    def _():
        o_ref[...] = _layernorm(ALPHA * x_ref[...] + acc_ref[...], lg_ref[...], lb_ref[...])


def _ff_tile(d_ff):
    half = d_ff // 2
    return half if half % LANES == 0 else d_ff


def _ffn(x, wg, wu, wd, ln_g, ln_b, tm=512):
    m = x.shape[0]
    d_ff = wg.shape[1]
    tm = min(tm, m)
    tf = _ff_tile(d_ff)
    const = lambda i, j: (0, 0)
    return pl.pallas_call(
        _ffn_kernel,
        grid=(m // tm, d_ff // tf),
        in_specs=[
            pl.BlockSpec((tm, D_MODEL), lambda i, j: (i, 0)),
            pl.BlockSpec((D_MODEL, tf), lambda i, j: (0, j)),
            pl.BlockSpec((D_MODEL, tf), lambda i, j: (0, j)),
            pl.BlockSpec((tf, D_MODEL), lambda i, j: (j, 0)),
            pl.BlockSpec((1, D_MODEL), const),
            pl.BlockSpec((1, D_MODEL), const),
        ],
        out_specs=pl.BlockSpec((tm, D_MODEL), lambda i, j: (i, 0)),
        out_shape=jax.ShapeDtypeStruct((m, D_MODEL), F32),
        scratch_shapes=[pltpu.VMEM((tm, D_MODEL), F32)],
        compiler_params=_params("arbitrary", "arbitrary"),
        name="ffn_ln",
    )(x, wg, wu, wd, ln_g, ln_b)


def _router_kernel(x_ref, rw_ref, c_ref):
    logits = jnp.dot(x_ref[...], rw_ref[...], precision=lax.Precision.HIGHEST, preferred_element_type=F32)
    lane = lax.broadcasted_iota(jnp.int32, logits.shape, 1)
    logits = jnp.where(lane < N_EXPERTS, logits, NEG)
    m1 = jnp.max(logits, axis=1, keepdims=True)
    i1 = jnp.min(jnp.where(logits == m1, lane, LANES), axis=1, keepdims=True)
    rest = jnp.where(lane == i1, NEG, logits)
    m2 = jnp.max(rest, axis=1, keepdims=True)
    i2 = jnp.min(jnp.where(rest == m2, lane, LANES), axis=1, keepdims=True)
    e2 = jnp.exp(m2 - m1)
    den = 1.0 + e2
    c_ref[...] = jnp.where(lane == i1, 1.0 / den, 0.0) + jnp.where(lane == i2, e2 / den, 0.0)


def _router(x, rw_pad, tm=512):
    m = x.shape[0]
    tm = min(tm, m)
    return pl.pallas_call(
        _router_kernel,
        grid=(m // tm,),
        in_specs=[pl.BlockSpec((tm, D_MODEL), lambda i: (i, 0)),
                  pl.BlockSpec((D_MODEL, LANES), lambda i: (0, 0))],
        out_specs=pl.BlockSpec((tm, LANES), lambda i: (i, 0)),
        out_shape=jax.ShapeDtypeStruct((m, LANES), F32),
        compiler_params=_params("arbitrary"),
        name="router",
    )(x, rw_pad)


def _moe_kernel(x_ref, c_ref, wg_ref, wu_ref, wd_ref, lg_ref, lb_ref, o_ref, acc_ref):
    e = pl.program_id(1)
    j = pl.program_id(2)
    xb = x_ref[...].astype(BF16)
    h = (_silu(_dot(xb, wg_ref[0])) * _dot(xb, wu_ref[0])).astype(BF16)
    comb = c_ref[...]
    lane = lax.broadcasted_iota(jnp.int32, comb.shape, 1)
    w_e = jnp.sum(jnp.where(lane == e, comb, 0.0), axis=1, keepdims=True)
    part = w_e * _dot(h, wd_ref[0])
    first = jnp.logical_and(e == 0, j == 0)

    @pl.when(first)
    def _():
        acc_ref[...] = part

    @pl.when(jnp.logical_not(first))
    def _():
        acc_ref[...] += part

    @pl.when(jnp.logical_and(e == pl.num_programs(1) - 1, j == pl.num_programs(2) - 1))
    def _():
        o_ref[...] = _layernorm(ALPHA * x_ref[...] + acc_ref[...], lg_ref[...], lb_ref[...])


def _moe(x, comb, wg, wu, wd, ln_g, ln_b, tm=512):
    m = x.shape[0]
    n_e, _, d_ff = wg.shape
    tm = min(tm, m)
    tf = _ff_tile(d_ff)
    const = lambda i, e, j: (0, 0)
    return pl.pallas_call(
        _moe_kernel,
        grid=(m // tm, n_e, d_ff // tf),
        in_specs=[
            pl.BlockSpec((tm, D_MODEL), lambda i, e, j: (i, 0)),
            pl.BlockSpec((tm, LANES), lambda i, e, j: (i, 0)),
            pl.BlockSpec((1, D_MODEL, tf), lambda i, e, j: (e, 0, j)),
            pl.BlockSpec((1, D_MODEL, tf), lambda i, e, j: (e, 0, j)),
            pl.BlockSpec((1, tf, D_MODEL), lambda i, e, j: (e, j, 0)),
            pl.BlockSpec((1, D_MODEL), const),
            pl.BlockSpec((1, D_MODEL), const),
        ],
        out_specs=pl.BlockSpec((tm, D_MODEL), lambda i, e, j: (i, 0)),
        out_shape=jax.ShapeDtypeStruct((m, D_MODEL), F32),
        scratch_shapes=[pltpu.VMEM((tm, D_MODEL), F32)],
        compiler_params=_params("arbitrary", "arbitrary", "arbitrary"),
        name="moe_ln",
    )(x, comb, wg, wu, wd, ln_g, ln_b)


def _layer_weights(l, w_in, conv_w, conv_b, dt_bias, a_log, d_skip, ssm_norm_g, q_norm_g, w_qb, kv_norm_g,
                   w_kvb, w_br_ssm, w_br_mla, w_out, ln1_g, ln1_b, ln2_g, ln2_b):
    wi = w_in[l]
    o_xbc = D_INNER
    o_dt = o_xbc + CONV_DIM
    o_qa = o_dt + SSM_HEADS
    o_kv = o_qa + Q_LORA
    o_gs = o_kv + KV_LORA + QK_ROPE
    o_gm = o_gs + D_MODEL
    w_big = jnp.concatenate([wi[:, o_xbc:o_dt], wi[:, 0:o_xbc], wi[:, o_gs:o_gm], wi[:, o_gm:]], axis=1).astype(BF16)
    w_small = jnp.concatenate(
        [wi[:, o_qa:o_kv], wi[:, o_kv:o_gs], wi[:, o_dt:o_qa],
         jnp.zeros((D_MODEL, LANES - QK_ROPE - SSM_HEADS), F32)], axis=1).astype(BF16)

    def tail_pad(v):
        return jnp.zeros((1, LANES), F32).at[0, DT_OFF:DT_OFF + SSM_HEADS].set(v)

    wq = w_qb[l].reshape(Q_LORA, MLA_HEADS, QK_NOPE + QK_ROPE)
    wkv = w_kvb[l].reshape(KV_LORA, MLA_HEADS, QK_NOPE + V_DIM)
    return {
        "w_big": w_big,
        "w_small": w_small,
        "conv_w": conv_w[l],
        "conv_b": conv_b[l].reshape(1, CONV_DIM),
        "dt_bias": tail_pad(dt_bias[l]),
        "a_log": tail_pad(a_log[l]),
        "d_exp": jnp.repeat(d_skip[l], SSM_HEADDIM).reshape(1, D_INNER),
        "ssm_norm_g": ssm_norm_g[l].reshape(1, D_INNER),
        "q_norm_g": q_norm_g[l].reshape(1, Q_LORA),
        "kv_norm_g": kv_norm_g[l].reshape(1, KV_LORA),
        "w_q_nope": wq[:, :, :QK_NOPE].reshape(Q_LORA, MLA_HEADS * QK_NOPE).astype(BF16),
        "w_q_rope": wq[:, :, QK_NOPE:].reshape(Q_LORA, MLA_HEADS * QK_ROPE).astype(BF16),
        "w_k_t": jnp.transpose(wkv[:, :, :QK_NOPE], (1, 2, 0)).astype(BF16),
        "w_v": jnp.transpose(wkv[:, :, QK_NOPE:], (1, 0, 2)).astype(BF16),
        "w_br_ssm": w_br_ssm[l].astype(BF16),
        "w_br_mla": w_br_mla[l].astype(BF16),
        "w_out": w_out[l].astype(BF16),
        "ln1_g": ln1_g[l].reshape(1, D_MODEL),
        "ln1_b": ln1_b[l].reshape(1, D_MODEL),
        "ln2_g": ln2_g[l].reshape(1, D_MODEL),
        "ln2_b": ln2_b[l].reshape(1, D_MODEL),
    }


def _rope_tables(pos):
    half = QK_ROPE // 2
    inv = ROPE_BASE ** (-jnp.arange(0, QK_ROPE, 2, dtype=F32) / QK_ROPE)
    ang = pos.astype(F32)[:, None] * inv[None, :]
    cos, sin = jnp.cos(ang), jnp.sin(ang)
    cos_t = jnp.tile(cos, (1, LANES // half))
    sin_t = jnp.tile(jnp.concatenate([-sin, sin], axis=1), (1, LANES // QK_ROPE))
    return cos_t, sin_t


def kernel(x_prompt, x_sample, cache_latent, cache_krope, state_ssm, state_conv, page_table, w_in, conv_w, conv_b, dt_bias, a_log, d_skip, ssm_norm_g, q_norm_g, w_qb, kv_norm_g, w_kvb, w_br_ssm, w_br_mla, w_out, ln1_g, ln1_b, ln2_g, ln2_b, ffn_w_gate, ffn_w_up, ffn_w_down, router_w, moe_w_gate, moe_w_up, moe_w_down):
    nb, t, _ = x_prompt.shape
    ns = x_sample.shape[0]
    assert x_sample.shape[1] == 1
    n_past = page_table.shape[1] * PAGE_SIZE
    cos_p, sin_p = _rope_tables(jnp.arange(t, dtype=jnp.int32))
    cos_s, sin_s = _rope_tables(jnp.full((ns,), n_past, dtype=jnp.int32))

    xp = x_prompt.reshape(nb * t, D_MODEL)
    xs = x_sample.reshape(ns, D_MODEL)
    outs = [[] for _ in range(8)]
    for l in range(DEPTH):
        lw = _layer_weights(l, w_in, conv_w, conv_b, dt_bias, a_log, d_skip, ssm_norm_g, q_norm_g, w_qb,
                            kv_norm_g, w_kvb, w_br_ssm, w_br_mla, w_out, ln1_g, ln1_b, ln2_g, ln2_b)
        big_p = _matmul(xp, lw["w_big"], BF16)
        small_p = _matmul(xp, lw["w_small"], F32)
        big_p3 = big_p.reshape(nb, t, -1)
        small_p3 = small_p.reshape(nb, t, SM_W)
        y_ssm_p, h_p, cv_p = _ssd_prompt(big_p3, small_p3, lw)
        qlat, qrope, kc, kr, lat_f, kr_f = _mla_prep(small_p3, cos_p, sin_p, lw)
        y_mla_p = _attn_prompt(qlat, qrope, kc, kr, lw["w_v"])
        xp = _merge(y_ssm_p.reshape(nb * t, D_INNER), y_mla_p.reshape(nb * t, -1), big_p, xp, lw)
        big_s = _matmul(xs, lw["w_big"], F32)
        small_s = _matmul(xs, lw["w_small"], F32)
        y_ssm_s, h_s, cv_s = _ssd_step(big_s.reshape(ns, 1, -1), small_s.reshape(ns, 1, SM_W),
                                       state_conv[l], state_ssm[l].reshape(ns, D_INNER, D_STATE), lw)
        qlat_s, qrope_s, kc_s, kr_s, lat_fs, kr_fs = _mla_prep(small_s.reshape(1, ns, SM_W), cos_s, sin_s, lw)
        o_lat = _attn_decode(l, page_table, cache_latent, cache_krope,
                             jnp.transpose(qlat_s[0], (1, 0, 2)), jnp.transpose(qrope_s[0], (1, 0, 2)),
                             kc_s.reshape(ns, 1, KV_LORA), kr_s.reshape(ns, 1, QK_ROPE))
        y_mla_s = _vproj(jnp.transpose(o_lat, (1, 0, 2)), lw["w_v"])
        xs = _merge(y_ssm_s.reshape(ns, D_INNER), y_mla_s, big_s, xs, lw)
        i = l // 2
        if l % 2 == 0:
            wg, wu, wd = ffn_w_gate[i].astype(BF16), ffn_w_up[i].astype(BF16), ffn_w_down[i].astype(BF16)
            xp = _ffn(xp, wg, wu, wd, lw["ln2_g"], lw["ln2_b"])
            xs = _ffn(xs, wg, wu, wd, lw["ln2_g"], lw["ln2_b"])
        else:
            wg, wu, wd = moe_w_gate[i].astype(BF16), moe_w_up[i].astype(BF16), moe_w_down[i].astype(BF16)
            rw = jnp.zeros((D_MODEL, LANES), F32).at[:, :N_EXPERTS].set(router_w[i])
            xp = _moe(xp, _router(xp, rw), wg, wu, wd, lw["ln2_g"], lw["ln2_b"])
            xs = _moe(xs, _router(xs, rw), wg, wu, wd, lw["ln2_g"], lw["ln2_b"])
        for k, v in enumerate((lat_f, kr_f, h_p.reshape(nb, SSM_HEADS, SSM_HEADDIM, D_STATE), cv_p,
                               lat_fs.reshape(ns, 1, KV_LORA), kr_fs.reshape(ns, 1, QK_ROPE),
                               h_s.reshape(ns, SSM_HEADS, SSM_HEADDIM, D_STATE), cv_s)):
            outs[k].append(v)

    return (xp.reshape(nb, t, D_MODEL), xs.reshape(ns, 1, D_MODEL)) + tuple(jnp.stack(o) for o in outs)
```

```python
import functools
import math

import jax
import jax.numpy as jnp
from jax import lax
from jax.experimental import pallas as pl
from jax.experimental.pallas import tpu as pltpu

F32 = jnp.float32
BF16 = jnp.bfloat16

D_MODEL = 1024
DEPTH = 4
PAGE_SIZE = 128
D_INNER = 2048
SSM_HEADDIM = 64
SSM_HEADS = 32
SSM_GROUPS = 8
HEADS_PER_GROUP = 4
D_STATE = 128
CONV_W = 4
CONV_DIM = 4096
SSD_CHUNK = 128
MLA_HEADS = 16
QK_NOPE = 128
QK_ROPE = 64
V_DIM = 128
Q_LORA = 256
KV_LORA = 256
ROPE_BASE = 10000.0
N_EXPERTS = 8
ALPHA = (2.0 * DEPTH) ** 0.25
LN_EPS = 1e-5
RMS_EPS = 1e-6

LANES = 128
SUBLANES = 8
VMEM_LIMIT = 56 * 1024 * 1024

SM_QA = 0
SM_LAT = Q_LORA
SM_TAIL = Q_LORA + KV_LORA
SM_W = SM_TAIL + LANES
DT_OFF = QK_ROPE
NEG = -1e30


def _dot(a, b):
    return jnp.dot(a, b, preferred_element_type=F32)


def _dot_nt(a, b):
    return lax.dot_general(a, b, (((1,), (1,)), ((), ())), preferred_element_type=F32)


def _dot_tn(a, b):
    return lax.dot_general(a, b, (((0,), (0,)), ((), ())), preferred_element_type=F32)


def _silu(x):
    return x * jax.nn.sigmoid(x)


def _softplus(x):
    return jnp.maximum(x, 0.0) + jnp.log1p(jnp.exp(-jnp.abs(x)))


def _params(*sem):
    return pltpu.CompilerParams(dimension_semantics=sem, vmem_limit_bytes=VMEM_LIMIT)


def _layernorm(r, g, b):
    mu = jnp.mean(r, axis=-1, keepdims=True)
    d = r - mu
    var = jnp.mean(d * d, axis=-1, keepdims=True)
    return d * lax.rsqrt(var + LN_EPS) * g + b


def _rms(x, g):
    return x * lax.rsqrt(jnp.mean(x * x, axis=-1, keepdims=True) + RMS_EPS) * g


def _mm_kernel(x_ref, w_ref, o_ref):
    o_ref[...] = _dot(x_ref[...].astype(BF16), w_ref[...]).astype(o_ref.dtype)


def _matmul(x, w, out_dtype, tm=1024, tn=1024):
    m, k = x.shape
    n = w.shape[1]
    tm = min(tm, m)
    tn = min(tn, n)
    return pl.pallas_call(
        _mm_kernel,
        grid=(m // tm, n // tn),
        in_specs=[pl.BlockSpec((tm, k), lambda i, j: (i, 0)),
                  pl.BlockSpec((k, tn), lambda i, j: (0, j))],
        out_specs=pl.BlockSpec((tm, tn), lambda i, j: (i, j)),
        out_shape=jax.ShapeDtypeStruct((m, n), out_dtype),
        compiler_params=_params("arbitrary", "arbitrary"),
        name="in_proj",
    )(x, w)


def _expand_heads(col_blk, lane, j):
    a = col_blk[:, DT_OFF + 2 * j:DT_OFF + 2 * j + 1]
    b = col_blk[:, DT_OFF + 2 * j + 1:DT_OFF + 2 * j + 2]
    return jnp.where(lane < SSM_HEADDIM, a, b)


def _ssd_kernel(xbc_ref, z_ref, sm_ref, cw_ref, cb_ref, dtb_ref, alog_ref, dexp_ref, ng_ref,
                y_ref, hout_ref, cout_ref, ext_ref, h_ref, ybuf_ref, *, L):
    c = pl.program_id(1)
    nc = pl.num_programs(1)
    rp = HEADS_PER_GROUP * SSM_HEADDIM
    gn = SSM_GROUPS * D_STATE
    hist = SUBLANES

    u = xbc_ref[0].astype(F32)

    @pl.when(c == 0)
    def _():
        ext_ref[0:hist, :] = jnp.zeros((hist, CONV_DIM), F32)
        h_ref[...] = jnp.zeros_like(h_ref)

    @pl.when(c > 0)
    def _():
        ext_ref[0:hist, :] = ext_ref[L:L + hist, :]

    ext_ref[hist:hist + L, :] = u
    acc = cb_ref[...] + u * cw_ref[CONV_W - 1:CONV_W, :]
    for s in range(1, CONV_W):
        acc = acc + ext_ref[hist - s:hist - s + L, :] * cw_ref[CONV_W - 1 - s:CONV_W - s, :]
    act = _silu(acc)

    tail = sm_ref[0, :, SM_TAIL:SM_TAIL + LANES]
    dt = _softplus(tail + dtb_ref[...])
    da = dt * (-jnp.exp(alog_ref[...]))
    row = lax.broadcasted_iota(jnp.int32, (L, L), 0)
    col = lax.broadcasted_iota(jnp.int32, (L, L), 1)
    causal = row >= col
    tri = causal.astype(F32)
    acum = jnp.dot(tri, da, precision=lax.Precision.HIGHEST, preferred_element_type=F32)
    acum_t = acum.T
    a_last = acum[L - 1:L, :]
    to_end = jnp.exp(a_last - acum)
    e_acum = jnp.exp(acum)
    cdec = jnp.exp(a_last)
    lane = lax.broadcasted_iota(jnp.int32, (L, LANES), 1)

    def expand_group(v, g):
        return jnp.concatenate([_expand_heads(v, lane, 2 * g), _expand_heads(v, lane, 2 * g + 1)], axis=1)

    for g in range(SSM_GROUPS):
        xs_g = act[:, g * rp:(g + 1) * rp]
        bb = act[:, D_INNER + g * D_STATE:D_INNER + (g + 1) * D_STATE].astype(BF16)
        cbf = act[:, D_INNER + gn + g * D_STATE:D_INNER + gn + (g + 1) * D_STATE].astype(BF16)
        xdt = xs_g * expand_group(dt, g)
        xdtb = xdt.astype(BF16)
        cb = _dot_nt(cbf, bb)
        yd = []
        for r in range(HEADS_PER_GROUP):
            hd = DT_OFF + g * HEADS_PER_GROUP + r
            seg = acum[:, hd:hd + 1] - acum_t[hd:hd + 1, :]
            dec = jnp.exp(jnp.where(causal, seg, NEG))
            yd.append(_dot((cb * dec).astype(BF16), xdtb[:, r * SSM_HEADDIM:(r + 1) * SSM_HEADDIM]))
        y_diag = jnp.concatenate(yd, axis=1)
        hg = h_ref[g * rp:(g + 1) * rp, :]
        y_off = _dot_nt(cbf, hg.astype(BF16)) * expand_group(e_acum, g)
        xw = (xdt * expand_group(to_end, g)).astype(BF16)
        st = _dot_tn(xw, bb)
        for r in range(HEADS_PER_GROUP):
            hd = DT_OFF + g * HEADS_PER_GROUP + r
            lo = g * rp + r * SSM_HEADDIM
            h_ref[lo:lo + SSM_HEADDIM, :] = (cdec[0:1, hd:hd + 1] * hg[r * SSM_HEADDIM:(r + 1) * SSM_HEADDIM, :]
                                             + st[r * SSM_HEADDIM:(r + 1) * SSM_HEADDIM, :])
        ybuf_ref[:, g * rp:(g + 1) * rp] = y_diag + y_off + xs_g * dexp_ref[:, g * rp:(g + 1) * rp]

    yz = ybuf_ref[...] * _silu(z_ref[0].astype(F32))
    y_ref[0] = _rms(yz, ng_ref[...]).astype(y_ref.dtype)

    @pl.when(c == nc - 1)
    def _():
        hout_ref[0] = h_ref[...]
        cout_ref[0] = ext_ref[hist + L - (CONV_W - 1):hist + L, :]


def _ssd_prompt(big, small, lw):
    nb, t, _ = big.shape
    L = SSD_CHUNK if t % SSD_CHUNK == 0 else t
    nc = t // L
    kern = functools.partial(_ssd_kernel, L=L)
    const = lambda b, c: (0, 0)
    return pl.pallas_call(
        kern,
        grid=(nb, nc),
        in_specs=[
            pl.BlockSpec((1, L, CONV_DIM), lambda b, c: (b, c, 0)),
            pl.BlockSpec((1, L, D_INNER), lambda b, c: (b, c, 2)),
            pl.BlockSpec((1, L, SM_W), lambda b, c: (b, c, 0)),
            pl.BlockSpec((CONV_W, CONV_DIM), const),
            pl.BlockSpec((1, CONV_DIM), const),
            pl.BlockSpec((1, LANES), const),
            pl.BlockSpec((1, LANES), const),
            pl.BlockSpec((1, D_INNER), const),
            pl.BlockSpec((1, D_INNER), const),
        ],
        out_specs=[
            pl.BlockSpec((1, L, D_INNER), lambda b, c: (b, c, 0)),
            pl.BlockSpec((1, D_INNER, D_STATE), lambda b, c: (b, 0, 0)),
            pl.BlockSpec((1, CONV_W - 1, CONV_DIM), lambda b, c: (b, 0, 0)),
        ],
        out_shape=[
            jax.ShapeDtypeStruct((nb, t, D_INNER), BF16),
            jax.ShapeDtypeStruct((nb, D_INNER, D_STATE), F32),
            jax.ShapeDtypeStruct((nb, CONV_W - 1, CONV_DIM), F32),
        ],
        scratch_shapes=[
            pltpu.VMEM((L + 2 * SUBLANES, CONV_DIM), F32),
            pltpu.VMEM((D_INNER, D_STATE), F32),
            pltpu.VMEM((L, D_INNER), F32),
        ],
        compiler_params=_params("arbitrary", "arbitrary"),
        name="ssd_prompt",
    )(big, big, small, lw["conv_w"], lw["conv_b"], lw["dt_bias"], lw["a_log"], lw["d_exp"], lw["ssm_norm_g"])


def _ssd_step_kernel(xbc_ref, z_ref, sm_ref, cs_ref, hs_ref, cw_ref, cb_ref, dtb_ref, alog_ref, dexp_ref, ng_ref,
                     y_ref, hout_ref, cout_ref):
    gn = SSM_GROUPS * D_STATE
    u = xbc_ref[0].astype(F32)
    prev = cs_ref[0]
    acc = cb_ref[...] + u * cw_ref[CONV_W - 1:CONV_W, :]
    for k in range(CONV_W - 1):
        acc = acc + prev[k:k + 1, :] * cw_ref[k:k + 1, :]
    cout_ref[0, 0:CONV_W - 2, :] = prev[1:CONV_W - 1, :]
    cout_ref[0, CONV_W - 2:CONV_W - 1, :] = u
    act = _silu(acc)

    tail = sm_ref[0, :, SM_TAIL:SM_TAIL + LANES]
    dt = _softplus(tail + dtb_ref[...])
    d_a = jnp.exp(dt * (-jnp.exp(alog_ref[...])))
    lane = lax.broadcasted_iota(jnp.int32, (1, LANES), 1)
    ys = []
    for j in range(D_INNER // LANES):
        g = (j * LANES) // (HEADS_PER_GROUP * SSM_HEADDIM)
        xs_j = act[:, j * LANES:(j + 1) * LANES]
        xdt = xs_j * _expand_heads(dt, lane, j)
        xcol = jnp.broadcast_to(xdt, (LANES, LANES)).T
        dcol = jnp.broadcast_to(_expand_heads(d_a, lane, j), (LANES, LANES)).T
        b_g = act[:, D_INNER + g * D_STATE:D_INNER + (g + 1) * D_STATE]
        c_g = act[:, D_INNER + gn + g * D_STATE:D_INNER + gn + (g + 1) * D_STATE]
        hnew = dcol * hs_ref[0, j * LANES:(j + 1) * LANES, :] + xcol * b_g
        hout_ref[0, j * LANES:(j + 1) * LANES, :] = hnew
        yrow = jnp.sum((hnew * c_g).T, axis=0, keepdims=True)
        ys.append(yrow + xs_j * dexp_ref[:, j * LANES:(j + 1) * LANES])
    y = jnp.concatenate(ys, axis=1)
    yz = y * _silu(z_ref[0].astype(F32))
    y_ref[0] = _rms(yz, ng_ref[...]).astype(y_ref.dtype)


def _ssd_step(layer, big, small, conv_state, ssm_state, lw):
    n = big.shape[0]
    const = lambda s: (0, 0)
    return pl.pallas_call(
        _ssd_step_kernel,
        grid=(n,),
        in_specs=[
            pl.BlockSpec((1, 1, CONV_DIM), lambda s: (s, 0, 0)),
            pl.BlockSpec((1, 1, D_INNER), lambda s: (s, 0, 2)),
            pl.BlockSpec((1, 1, SM_W), lambda s: (s, 0, 0)),
            pl.BlockSpec((1, CONV_W - 1, CONV_DIM), lambda s: (s, 0, 0)),
            pl.BlockSpec((None, 1, D_INNER, D_STATE), lambda s: (layer, s, 0, 0)),
            pl.BlockSpec((CONV_W, CONV_DIM), const),
            pl.BlockSpec((1, CONV_DIM), const),
            pl.BlockSpec((1, LANES), const),
            pl.BlockSpec((1, LANES), const),
            pl.BlockSpec((1, D_INNER), const),
            pl.BlockSpec((1, D_INNER), const),
        ],
        out_specs=[
            pl.BlockSpec((1, 1, D_INNER), lambda s: (s, 0, 0)),
            pl.BlockSpec((1, D_INNER, D_STATE), lambda s: (s, 0, 0)),
            pl.BlockSpec((1, CONV_W - 1, CONV_DIM), lambda s: (s, 0, 0)),
        ],
        out_shape=[
            jax.ShapeDtypeStruct((n, 1, D_INNER), BF16),
            jax.ShapeDtypeStruct((n, D_INNER, D_STATE), F32),
            jax.ShapeDtypeStruct((n, CONV_W - 1, CONV_DIM), F32),
        ],
        compiler_params=_params("arbitrary"),
        name="ssd_step",
    )(big, big, small, conv_state, ssm_state, lw["conv_w"], lw["conv_b"], lw["dt_bias"], lw["a_log"],
      lw["d_exp"], lw["ssm_norm_g"])


def _prep_kernel(sm_ref, cos_ref, sin_ref, qg_ref, kvg_ref, wqn_ref, wqr_ref, wkt_ref,
                 qlat_ref, qrope_ref, kc_ref, kr_ref, latf_ref, krf_ref, kct_ref, krt_ref, *, scale):
    sm = sm_ref[0]
    tm = sm.shape[0]
    half = QK_ROPE // 2
    qn = _rms(sm[:, SM_QA:SM_QA + Q_LORA], qg_ref[...]).astype(BF16)
    qnope = _dot(qn, wqn_ref[...])
    qr = _dot(qn, wqr_ref[...])
    cos = cos_ref[...]
    sin = sin_ref[...]
    wq = MLA_HEADS * QK_ROPE
    reps = wq // LANES
    cosq = jnp.concatenate([cos] * reps, axis=1)
    sinq = jnp.concatenate([sin] * reps, axis=1)
    lane_q = lax.broadcasted_iota(jnp.int32, (tm, wq), 1)
    swap = jnp.where((lane_q & (QK_ROPE - 1)) < half, pltpu.roll(qr, wq - half, 1), pltpu.roll(qr, half, 1))
    qrot = (qr * cosq + swap * sinq) * scale
    for h in range(MLA_HEADS):
        qrope_ref[0, h] = qrot[:, h * QK_ROPE:(h + 1) * QK_ROPE].astype(BF16)
        ql = _dot(qnope[:, h * QK_NOPE:(h + 1) * QK_NOPE].astype(BF16), wkt_ref[h]) * scale
        qlat_ref[0, h] = ql.astype(BF16)
    cn = _rms(sm[:, SM_LAT:SM_LAT + KV_LORA], kvg_ref[...])
    latf_ref[0] = cn
    kc_ref[0] = cn.astype(BF16)
    kct_ref[0] = cn.T.astype(BF16)
    tail = sm[:, SM_TAIL:SM_TAIL + LANES]
    lane_k = lax.broadcasted_iota(jnp.int32, (tm, LANES), 1)
    swapk = jnp.where(lane_k < half, pltpu.roll(tail, LANES - half, 1), pltpu.roll(tail, half, 1))
    krot = tail * cos + swapk * sin
    krf_ref[0] = krot[:, 0:QK_ROPE]
    kr_ref[0] = krot[:, 0:QK_ROPE].astype(BF16)
    krt_ref[0] = krot.T[0:QK_ROPE, :].astype(BF16)


def _mla_prep(small, cos, sin, lw, tm=256):
    nb, t, _ = small.shape
    tm = min(tm, t)
    scale = (QK_NOPE + QK_ROPE) ** -0.5 * math.log2(math.e)
    const2 = lambda b, i: (0, 0)
    return pl.pallas_call(
        functools.partial(_prep_kernel, scale=scale),
        grid=(nb, t // tm),
        in_specs=[
            pl.BlockSpec((1, tm, SM_W), lambda b, i: (b, i, 0)),
            pl.BlockSpec((tm, LANES), lambda b, i: (i, 0)),
            pl.BlockSpec((tm, LANES), lambda b, i: (i, 0)),
            pl.BlockSpec((1, Q_LORA), const2),
            pl.BlockSpec((1, KV_LORA), const2),
            pl.BlockSpec((Q_LORA, MLA_HEADS * QK_NOPE), const2),
            pl.BlockSpec((Q_LORA, MLA_HEADS * QK_ROPE), const2),
            pl.BlockSpec((MLA_HEADS, QK_NOPE, KV_LORA), lambda b, i: (0, 0, 0)),
        ],
        out_specs=[
            pl.BlockSpec((1, MLA_HEADS, tm, KV_LORA), lambda b, i: (b, 0, i, 0)),
            pl.BlockSpec((1, MLA_HEADS, tm, QK_ROPE), lambda b, i: (b, 0, i, 0)),
            pl.BlockSpec((1, tm, KV_LORA), lambda b, i: (b, i, 0)),
            pl.BlockSpec((1, tm, QK_ROPE), lambda b, i: (b, i, 0)),
            pl.BlockSpec((1, tm, KV_LORA), lambda b, i: (b, i, 0)),
            pl.BlockSpec((1, tm, QK_ROPE), lambda b, i: (b, i, 0)),
            pl.BlockSpec((1, KV_LORA, tm), lambda b, i: (b, 0, i)),
            pl.BlockSpec((1, QK_ROPE, tm), lambda b, i: (b, 0, i)),
        ],
        out_shape=[
            jax.ShapeDtypeStruct((nb, MLA_HEADS, t, KV_LORA), BF16),
            jax.ShapeDtypeStruct((nb, MLA_HEADS, t, QK_ROPE), BF16),
            jax.ShapeDtypeStruct((nb, t, KV_LORA), BF16),
            jax.ShapeDtypeStruct((nb, t, QK_ROPE), BF16),
            jax.ShapeDtypeStruct((nb, t, KV_LORA), F32),
            jax.ShapeDtypeStruct((nb, t, QK_ROPE), F32),
            jax.ShapeDtypeStruct((nb, KV_LORA, t), BF16),
            jax.ShapeDtypeStruct((nb, QK_ROPE, t), BF16),
        ],
        compiler_params=_params("arbitrary", "arbitrary"),
        name="mla_prep",
    )(small, cos, sin, lw["q_norm_g"], lw["kv_norm_g"], lw["w_q_nope"], lw["w_q_rope"], lw["w_k_t"])


def _lane_tiles(x):
    return [x[:, k * LANES:(k + 1) * LANES] for k in range(x.shape[1] // LANES)]


def _softmax_step(s, m_old=None, l_old=None):
    tiles = _lane_tiles(s)
    rows = s.shape[0]
    row_max = jnp.broadcast_to(jnp.max(functools.reduce(jnp.maximum, tiles), axis=1, keepdims=True), (rows, LANES))
    m_new = row_max if m_old is None else jnp.maximum(m_old, row_max)
    p_tiles = [jnp.exp2(t - m_new) for t in tiles]
    l_new = jnp.broadcast_to(jnp.sum(functools.reduce(jnp.add, p_tiles), axis=1, keepdims=True), (rows, LANES))
    alpha = None
    if m_old is not None:
        alpha = jnp.exp2(m_old - m_new)
        l_new = alpha * l_old + l_new
    return m_new, alpha, l_new, jnp.concatenate(p_tiles, axis=1)


def _attn_kernel(qlat_ref, qrope_ref, kc_ref, kct_ref, krt_ref, wv_ref, o_ref, m_ref, l_ref, acc_ref, *, tq, tk, hc):
    i = pl.program_id(1)
    cr = hc * tq
    reps = KV_LORA // LANES

    def block(j, diagonal):
        start = pl.multiple_of(j * tk, tk)
        kc = kc_ref[0, pl.ds(start, tk), :]
        kct = kct_ref[0, :, pl.ds(start, tk)]
        krt = krt_ref[0, :, pl.ds(start, tk)]
        if diagonal:
            qpos = i * tq + lax.broadcasted_iota(jnp.int32, (hc, tq, tk), 1).reshape(cr, tk)
            kpos = j * tk + lax.broadcasted_iota(jnp.int32, (cr, tk), 1)
            visible = kpos <= qpos
        for c in range(MLA_HEADS // hc):
            r0 = c * cr
            q1 = qlat_ref[0, c * hc:(c + 1) * hc].reshape(cr, KV_LORA)
            q2 = qrope_ref[0, c * hc:(c + 1) * hc].reshape(cr, QK_ROPE)
            s = _dot(q1, kct) + _dot(q2, krt)
            if diagonal:
                s = jnp.where(visible, s, NEG)
                m_old = l_old = None
            else:
                m_old = m_ref[r0:r0 + cr, :]
                l_old = l_ref[r0:r0 + cr, :]
            m_new, alpha, l_new, p = _softmax_step(s, m_old, l_old)
            l_ref[r0:r0 + cr, :] = l_new
            m_ref[r0:r0 + cr, :] = m_new
            pv = _dot(p.astype(BF16), kc)
            if diagonal:
                acc_ref[r0:r0 + cr, :] = pv
            else:
                acc_ref[r0:r0 + cr, :] = jnp.concatenate([alpha] * reps, axis=1) * acc_ref[r0:r0 + cr, :] + pv

    nfull = (i * tq) // tk
    block(nfull, True)

    def body(j, carry):
        block(j, False)
        return carry

    lax.fori_loop(0, nfull, body, 0)
    for h in range(MLA_HEADS):
        l_h = jnp.concatenate([l_ref[h * tq:(h + 1) * tq, :]] * reps, axis=1)
        o_lat = (acc_ref[h * tq:(h + 1) * tq, :] / l_h).astype(BF16)
        o_ref[0, :, h * V_DIM:(h + 1) * V_DIM] = _dot(o_lat, wv_ref[h]).astype(o_ref.dtype)


def _attn_prompt(qlat, qrope, kc, kct, krt, wv, tq=128, tk=256, hc=16):
    nb, _, t, _ = qlat.shape
    tk = min(tk, t)
    tq = min(tq, tk)
    rows = MLA_HEADS * tq
    return pl.pallas_call(
        functools.partial(_attn_kernel, tq=tq, tk=tk, hc=hc),
        grid=(nb, t // tq),
        in_specs=[
            pl.BlockSpec((1, MLA_HEADS, tq, KV_LORA), lambda b, i: (b, 0, i, 0)),
            pl.BlockSpec((1, MLA_HEADS, tq, QK_ROPE), lambda b, i: (b, 0, i, 0)),
            pl.BlockSpec((1, t, KV_LORA), lambda b, i: (b, 0, 0)),
            pl.BlockSpec((1, KV_LORA, t), lambda b, i: (b, 0, 0)),
            pl.BlockSpec((1, QK_ROPE, t), lambda b, i: (b, 0, 0)),
            pl.BlockSpec((MLA_HEADS, KV_LORA, V_DIM), lambda b, i: (0, 0, 0)),
        ],
        out_specs=pl.BlockSpec((1, tq, MLA_HEADS * V_DIM), lambda b, i: (b, i, 0)),
        out_shape=jax.ShapeDtypeStruct((nb, t, MLA_HEADS * V_DIM), BF16),
        scratch_shapes=[pltpu.VMEM((rows, LANES), F32), pltpu.VMEM((rows, LANES), F32),
                        pltpu.VMEM((rows, KV_LORA), F32)],
        compiler_params=_params("arbitrary", "arbitrary"),
        name="attn_prompt",
    )(qlat, qrope, kc, kct, krt, wv)


def _decode_kernel(pt_ref, qlat_ref, qrope_ref, kcn_ref, krn_ref, *refs, n_pages):
    lat_refs = refs[:n_pages]
    krt_refs = refs[n_pages:2 * n_pages]
    o_ref, m_ref, l_ref, acc_ref = refs[2 * n_pages:]
    del pt_ref
    pc = pl.program_id(1)
    reps = KV_LORA // LANES
    q1 = qlat_ref[0]
    q2 = qrope_ref[0]

    @pl.when(pc == 0)
    def _():
        kn = kcn_ref[0].astype(F32)
        s_new = (jnp.sum(q1.astype(F32) * kn, axis=1, keepdims=True)
                 + jnp.sum(q2.astype(F32) * krn_ref[0].astype(F32), axis=1, keepdims=True))
        m_ref[...] = jnp.broadcast_to(s_new, m_ref.shape)
        l_ref[...] = jnp.ones(l_ref.shape, F32)
        acc_ref[...] = jnp.broadcast_to(kn, acc_ref.shape)

    lat = jnp.concatenate([r[0, 0] for r in lat_refs], axis=0).astype(BF16)
    krt = jnp.concatenate([r[0, 0] for r in krt_refs], axis=1).astype(BF16)
    s = _dot_nt(q1, lat) + _dot(q2, krt)
    m_new, alpha, l_new, p = _softmax_step(s, m_ref[...], l_ref[...])
    l_ref[...] = l_new
    m_ref[...] = m_new
    acc_ref[...] = jnp.concatenate([alpha] * reps, axis=1) * acc_ref[...] + _dot(p.astype(BF16), lat)

    @pl.when(pc == pl.num_programs(1) - 1)
    def _():
        o_ref[0] = (acc_ref[...] / jnp.concatenate([l_ref[...]] * reps, axis=1)).astype(o_ref.dtype)


def _attn_decode(layer, page_table, cache_latent, cache_krope_t, qlat, qrope, kcn, krn, pages_per_step=32):
    n, npages = page_table.shape
    n_pg = math.gcd(pages_per_step, npages)
    pt = page_table.reshape(-1)

    def page_map(k):
        return lambda s, pc, pt_ref: (layer, pt_ref[s * npages + pc * n_pg + k], 0, 0)

    row = lambda s, pc, pt_ref: (s, 0, 0)
    in_specs = [
        pl.BlockSpec((1, MLA_HEADS, KV_LORA), row),
        pl.BlockSpec((1, MLA_HEADS, QK_ROPE), row),
        pl.BlockSpec((1, 1, KV_LORA), row),
        pl.BlockSpec((1, 1, QK_ROPE), row),
    ]
    in_specs += [pl.BlockSpec((1, 1, PAGE_SIZE, KV_LORA), page_map(k)) for k in range(n_pg)]
    in_specs += [pl.BlockSpec((1, 1, QK_ROPE, PAGE_SIZE), page_map(k)) for k in range(n_pg)]
    grid_spec = pltpu.PrefetchScalarGridSpec(
        num_scalar_prefetch=1,
        grid=(n, npages // n_pg),
        in_specs=in_specs,
        out_specs=pl.BlockSpec((1, MLA_HEADS, KV_LORA), row),
        scratch_shapes=[pltpu.VMEM((MLA_HEADS, LANES), F32), pltpu.VMEM((MLA_HEADS, LANES), F32),
                        pltpu.VMEM((MLA_HEADS, KV_LORA), F32)],
    )
    return pl.pallas_call(
        functools.partial(_decode_kernel, n_pages=n_pg),
        grid_spec=grid_spec,
        out_shape=jax.ShapeDtypeStruct((n, MLA_HEADS, KV_LORA), BF16),
        compiler_params=_params("arbitrary", "arbitrary"),
        name="attn_decode",
    )(pt, qlat, qrope, kcn, krn, *([cache_latent] * n_pg), *([cache_krope_t] * n_pg))


def _vproj_kernel(o_ref, wv_ref, y_ref):
    for h in range(MLA_HEADS):
        y_ref[:, h * V_DIM:(h + 1) * V_DIM] = _dot(o_ref[h], wv_ref[h]).astype(y_ref.dtype)


def _vproj(o_lat, wv):
    n = o_lat.shape[1]
    return pl.pallas_call(
        _vproj_kernel,
        out_shape=jax.ShapeDtypeStruct((n, MLA_HEADS * V_DIM), BF16),
        name="v_proj",
    )(o_lat, wv)


def _merge_kernel(ys_ref, ym_ref, g_ref, x_ref, wbs_ref, wbm_ref, wo_ref, lg_ref, lb_ref, o_ref):
    gates = g_ref[...].astype(F32)
    a = _dot(ys_ref[...], wbs_ref[...])
    b = _dot(ym_ref[...], wbm_ref[...])
    m = jax.nn.sigmoid(gates[:, 0:D_MODEL]) * a + jax.nn.sigmoid(gates[:, D_MODEL:2 * D_MODEL]) * b
    r = ALPHA * x_ref[...] + _dot(m.astype(BF16), wo_ref[...])
    o_ref[...] = _layernorm(r, lg_ref[...], lb_ref[...])


def _merge(y_ssm, y_mla, big, x, lw, tm=512):
    m = x.shape[0]
    tm = min(tm, m)
    const = lambda i: (0, 0)
    return pl.pallas_call(
        _merge_kernel,
        grid=(m // tm,),
        in_specs=[
            pl.BlockSpec((tm, D_INNER), lambda i: (i, 0)),
            pl.BlockSpec((tm, MLA_HEADS * V_DIM), lambda i: (i, 0)),
            pl.BlockSpec((tm, 2 * D_MODEL), lambda i: (i, 3)),
            pl.BlockSpec((tm, D_MODEL), lambda i: (i, 0)),
            pl.BlockSpec((D_INNER, D_MODEL), const),
            pl.BlockSpec((MLA_HEADS * V_DIM, D_MODEL), const),
            pl.BlockSpec((D_MODEL, D_MODEL), const),
            pl.BlockSpec((1, D_MODEL), const),
            pl.BlockSpec((1, D_MODEL), const),
        ],
        out_specs=pl.BlockSpec((tm, D_MODEL), lambda i: (i, 0)),
        out_shape=jax.ShapeDtypeStruct((m, D_MODEL), F32),
        compiler_params=_params("arbitrary"),
        name="merge_ln",
    )(y_ssm, y_mla, big, x, lw["w_br_ssm"], lw["w_br_mla"], lw["w_out"], lw["ln1_g"], lw["ln1_b"])


def _ffn_kernel(x_ref, wg_ref, wu_ref, wd_ref, lg_ref, lb_ref, o_ref, acc_ref):
    j = pl.program_id(1)
    xb = x_ref[...].astype(BF16)
    h = (_silu(_dot(xb, wg_ref[...])) * _dot(xb, wu_ref[...])).astype(BF16)
    part = _dot(h, wd_ref[...])

    @pl.when(j == 0)
    def _():
        acc_ref[...] = part

    @pl.when(j > 0)
    def _():
        acc_ref[...] += part

    @pl.when(j == pl.num_programs(1) - 1)
    def _():
        o_ref[...] = _layernorm(ALPHA * x_ref[...] + acc_ref[...], lg_ref[...], lb_ref[...])


def _ff_tile(d_ff):
    half = d_ff // 2
    return half if half % LANES == 0 else d_ff


def _ffn(x, wg, wu, wd, ln_g, ln_b, tm=512):
    m = x.shape[0]
    d_ff = wg.shape[1]
    tm = min(tm, m)
    tf = _ff_tile(d_ff)
    const = lambda i, j: (0, 0)
    return pl.pallas_call(
        _ffn_kernel,
        grid=(m // tm, d_ff // tf),
        in_specs=[
            pl.BlockSpec((tm, D_MODEL), lambda i, j: (i, 0)),
            pl.BlockSpec((D_MODEL, tf), lambda i, j: (0, j)),
            pl.BlockSpec((D_MODEL, tf), lambda i, j: (0, j)),
            pl.BlockSpec((tf, D_MODEL), lambda i, j: (j, 0)),
            pl.BlockSpec((1, D_MODEL), const),
            pl.BlockSpec((1, D_MODEL), const),
        ],
        out_specs=pl.BlockSpec((tm, D_MODEL), lambda i, j: (i, 0)),
        out_shape=jax.ShapeDtypeStruct((m, D_MODEL), F32),
        scratch_shapes=[pltpu.VMEM((tm, D_MODEL), F32)],
        compiler_params=_params("arbitrary", "arbitrary"),
        name="ffn_ln",
    )(x, wg, wu, wd, ln_g, ln_b)


def _router_kernel(x_ref, rw_ref, c_ref):
    logits = jnp.dot(x_ref[...], rw_ref[...], precision=lax.Precision.HIGHEST, preferred_element_type=F32)
    lane = lax.broadcasted_iota(jnp.int32, logits.shape, 1)
    logits = jnp.where(lane < N_EXPERTS, logits, NEG)
    m1 = jnp.max(logits, axis=1, keepdims=True)
    i1 = jnp.min(jnp.where(logits == m1, lane, LANES), axis=1, keepdims=True)
    rest = jnp.where(lane == i1, NEG, logits)
    m2 = jnp.max(rest, axis=1, keepdims=True)
    i2 = jnp.min(jnp.where(rest == m2, lane, LANES), axis=1, keepdims=True)
    e2 = jnp.exp(m2 - m1)
    den = 1.0 + e2
    c_ref[...] = jnp.where(lane == i1, 1.0 / den, 0.0) + jnp.where(lane == i2, e2 / den, 0.0)


def _router(x, rw_pad, tm=512):
    m = x.shape[0]
    tm = min(tm, m)
    return pl.pallas_call(
        _router_kernel,
        grid=(m // tm,),
        in_specs=[pl.BlockSpec((tm, D_MODEL), lambda i: (i, 0)),
                  pl.BlockSpec((D_MODEL, LANES), lambda i: (0, 0))],
        out_specs=pl.BlockSpec((tm, LANES), lambda i: (i, 0)),
        out_shape=jax.ShapeDtypeStruct((m, LANES), F32),
        compiler_params=_params("arbitrary"),
        name="router",
    )(x, rw_pad)


def _moe_kernel(x_ref, c_ref, wg_ref, wu_ref, wd_ref, lg_ref, lb_ref, o_ref, acc_ref):
    e = pl.program_id(1)
    j = pl.program_id(2)
    xb = x_ref[...].astype(BF16)
    h = (_silu(_dot(xb, wg_ref[0])) * _dot(xb, wu_ref[0])).astype(BF16)
    comb = c_ref[...]
    lane = lax.broadcasted_iota(jnp.int32, comb.shape, 1)
    w_e = jnp.sum(jnp.where(lane == e, comb, 0.0), axis=1, keepdims=True)
    part = w_e * _dot(h, wd_ref[0])
    first = jnp.logical_and(e == 0, j == 0)

    @pl.when(first)
    def _():
        acc_ref[...] = part

    @pl.when(jnp.logical_not(first))
    def _():
        acc_ref[...] += part

    @pl.when(jnp.logical_and(e == pl.num_programs(1) - 1, j == pl.num_programs(2) - 1))
    def _():
        o_ref[...] = _layernorm(ALPHA * x_ref[...] + acc_ref[...], lg_ref[...], lb_ref[...])


def _moe(x, comb, wg, wu, wd, ln_g, ln_b, tm=512):
    m = x.shape[0]
    n_e, _, d_ff = wg.shape
    tm = min(tm, m)
    tf = _ff_tile(d_ff)
    const = lambda i, e, j: (0, 0)
    return pl.pallas_call(
        _moe_kernel,
        grid=(m // tm, n_e, d_ff // tf),
        in_specs=[
            pl.BlockSpec((tm, D_MODEL), lambda i, e, j: (i, 0)),
            pl.BlockSpec((tm, LANES), lambda i, e, j: (i, 0)),
            pl.BlockSpec((1, D_MODEL, tf), lambda i, e, j: (e, 0, j)),
            pl.BlockSpec((1, D_MODEL, tf), lambda i, e, j: (e, 0, j)),
            pl.BlockSpec((1, tf, D_MODEL), lambda i, e, j: (e, j, 0)),
            pl.BlockSpec((1, D_MODEL), const),
            pl.BlockSpec((1, D_MODEL), const),
        ],
        out_specs=pl.BlockSpec((tm, D_MODEL), lambda i, e, j: (i, 0)),
        out_shape=jax.ShapeDtypeStruct((m, D_MODEL), F32),
        scratch_shapes=[pltpu.VMEM((tm, D_MODEL), F32)],
        compiler_params=_params("arbitrary", "arbitrary", "arbitrary"),
        name="moe_ln",
    )(x, comb, wg, wu, wd, ln_g, ln_b)


def _layer_weights(l, w_in, conv_w, conv_b, dt_bias, a_log, d_skip, ssm_norm_g, q_norm_g, w_qb, kv_norm_g,
                   w_kvb, w_br_ssm, w_br_mla, w_out, ln1_g, ln1_b, ln2_g, ln2_b):
    wi = w_in[l]
    o_xbc = D_INNER
    o_dt = o_xbc + CONV_DIM
    o_qa = o_dt + SSM_HEADS
    o_kv = o_qa + Q_LORA
    o_gs = o_kv + KV_LORA + QK_ROPE
    o_gm = o_gs + D_MODEL
    w_big = jnp.concatenate([wi[:, o_xbc:o_dt], wi[:, 0:o_xbc], wi[:, o_gs:o_gm], wi[:, o_gm:]], axis=1).astype(BF16)
    w_small = jnp.concatenate(
        [wi[:, o_qa:o_kv], wi[:, o_kv:o_gs], wi[:, o_dt:o_qa],
         jnp.zeros((D_MODEL, LANES - QK_ROPE - SSM_HEADS), F32)], axis=1).astype(BF16)

    def tail_pad(v):
        return jnp.zeros((1, LANES), F32).at[0, DT_OFF:DT_OFF + SSM_HEADS].set(v)

    wq = w_qb[l].reshape(Q_LORA, MLA_HEADS, QK_NOPE + QK_ROPE)
    wkv = w_kvb[l].reshape(KV_LORA, MLA_HEADS, QK_NOPE + V_DIM)
    return {
        "w_big": w_big,
        "w_small": w_small,
        "conv_w": conv_w[l],
        "conv_b": conv_b[l].reshape(1, CONV_DIM),
        "dt_bias": tail_pad(dt_bias[l]),
        "a_log": tail_pad(a_log[l]),
        "d_exp": jnp.repeat(d_skip[l], SSM_HEADDIM).reshape(1, D_INNER),
        "ssm_norm_g": ssm_norm_g[l].reshape(1, D_INNER),
        "q_norm_g": q_norm_g[l].reshape(1, Q_LORA),
        "kv_norm_g": kv_norm_g[l].reshape(1, KV_LORA),
        "w_q_nope": wq[:, :, :QK_NOPE].reshape(Q_LORA, MLA_HEADS * QK_NOPE).astype(BF16),
        "w_q_rope": wq[:, :, QK_NOPE:].reshape(Q_LORA, MLA_HEADS * QK_ROPE).astype(BF16),
        "w_k_t": jnp.transpose(wkv[:, :, :QK_NOPE], (1, 2, 0)).astype(BF16),
        "w_v": jnp.transpose(wkv[:, :, QK_NOPE:], (1, 0, 2)).astype(BF16),
        "w_br_ssm": w_br_ssm[l].astype(BF16),
        "w_br_mla": w_br_mla[l].astype(BF16),
        "w_out": w_out[l].astype(BF16),
        "ln1_g": ln1_g[l].reshape(1, D_MODEL),
        "ln1_b": ln1_b[l].reshape(1, D_MODEL),
        "ln2_g": ln2_g[l].reshape(1, D_MODEL),
        "ln2_b": ln2_b[l].reshape(1, D_MODEL),
    }


def _rope_tables(pos):
    half = QK_ROPE // 2
    inv = ROPE_BASE ** (-jnp.arange(0, QK_ROPE, 2, dtype=F32) / QK_ROPE)
    ang = pos.astype(F32)[:, None] * inv[None, :]
    cos, sin = jnp.cos(ang), jnp.sin(ang)
    cos_t = jnp.tile(cos, (1, LANES // half))
    sin_t = jnp.tile(jnp.concatenate([-sin, sin], axis=1), (1, LANES // QK_ROPE))
    return cos_t, sin_t


def kernel(x_prompt, x_sample, cache_latent, cache_krope, state_ssm, state_conv, page_table, w_in, conv_w, conv_b, dt_bias, a_log, d_skip, ssm_norm_g, q_norm_g, w_qb, kv_norm_g, w_kvb, w_br_ssm, w_br_mla, w_out, ln1_g, ln1_b, ln2_g, ln2_b, ffn_w_gate, ffn_w_up, ffn_w_down, router_w, moe_w_gate, moe_w_up, moe_w_down):
    nb, t, _ = x_prompt.shape
    ns = x_sample.shape[0]
    assert x_sample.shape[1] == 1
    n_past = page_table.shape[1] * PAGE_SIZE
    cos_p, sin_p = _rope_tables(jnp.arange(t, dtype=jnp.int32))
    cos_s, sin_s = _rope_tables(jnp.full((ns,), n_past, dtype=jnp.int32))

    cache_krope_t = jnp.swapaxes(cache_krope, 2, 3)
    state_ssm_r = state_ssm.reshape(DEPTH, ns, D_INNER, D_STATE)

    xp = x_prompt.reshape(nb * t, D_MODEL)
    xs = x_sample.reshape(ns, D_MODEL)
    outs = [[] for _ in range(8)]
    for l in range(DEPTH):
        lw = _layer_weights(l, w_in, conv_w, conv_b, dt_bias, a_log, d_skip, ssm_norm_g, q_norm_g, w_qb,
                            kv_norm_g, w_kvb, w_br_ssm, w_br_mla, w_out, ln1_g, ln1_b, ln2_g, ln2_b)
        big_p = _matmul(xp, lw["w_big"], BF16)
        small_p = _matmul(xp, lw["w_small"], F32)
        big_p3 = big_p.reshape(nb, t, -1)
        small_p3 = small_p.reshape(nb, t, SM_W)
        y_ssm_p, h_p, cv_p = _ssd_prompt(big_p3, small_p3, lw)
        qlat, qrope, kc, _, lat_f, kr_f, kct, krt = _mla_prep(small_p3, cos_p, sin_p, lw)
        y_mla_p = _attn_prompt(qlat, qrope, kc, kct, krt, lw["w_v"])
        xp = _merge(y_ssm_p.reshape(nb * t, D_INNER), y_mla_p.reshape(nb * t, -1), big_p, xp, lw)
        big_s = _matmul(xs, lw["w_big"], F32)
        small_s = _matmul(xs, lw["w_small"], F32)
        y_ssm_s, h_s, cv_s = _ssd_step(l, big_s.reshape(ns, 1, -1), small_s.reshape(ns, 1, SM_W),
                                       state_conv[l], state_ssm_r, lw)
        qlat_s, qrope_s, kc_s, kr_s, lat_fs, kr_fs, _, _ = _mla_prep(small_s.reshape(1, ns, SM_W), cos_s, sin_s, lw)
        o_lat = _attn_decode(l, page_table, cache_latent, cache_krope_t,
                             jnp.transpose(qlat_s[0], (1, 0, 2)), jnp.transpose(qrope_s[0], (1, 0, 2)),
                             kc_s.reshape(ns, 1, KV_LORA), kr_s.reshape(ns, 1, QK_ROPE))
        y_mla_s = _vproj(jnp.transpose(o_lat, (1, 0, 2)), lw["w_v"])
        xs = _merge(y_ssm_s.reshape(ns, D_INNER), y_mla_s, big_s, xs, lw)
        i = l // 2
        if l % 2 == 0:
            wg, wu, wd = ffn_w_gate[i].astype(BF16), ffn_w_up[i].astype(BF16), ffn_w_down[i].astype(BF16)
            xp = _ffn(xp, wg, wu, wd, lw["ln2_g"], lw["ln2_b"])
            xs = _ffn(xs, wg, wu, wd, lw["ln2_g"], lw["ln2_b"])
        else:
            wg, wu, wd = moe_w_gate[i].astype(BF16), moe_w_up[i].astype(BF16), moe_w_down[i].astype(BF16)
            rw = jnp.zeros((D_MODEL, LANES), F32).at[:, :N_EXPERTS].set(router_w[i])
            xp = _moe(xp, _router(xp, rw), wg, wu, wd, lw["ln2_g"], lw["ln2_b"])
            xs = _moe(xs, _router(xs, rw), wg, wu, wd, lw["ln2_g"], lw["ln2_b"])
        for k, v in enumerate((lat_f, kr_f, h_p.reshape(nb, SSM_HEADS, SSM_HEADDIM, D_STATE), cv_p,
                               lat_fs.reshape(ns, 1, KV_LORA), kr_fs.reshape(ns, 1, QK_ROPE),
                               h_s.reshape(ns, SSM_HEADS, SSM_HEADDIM, D_STATE), cv_s)):
            outs[k].append(v)

    return (xp.reshape(nb, t, D_MODEL), xs.reshape(ns, 1, D_MODEL)) + tuple(jnp.stack(o) for o in outs)
```

```python
import functools
import math

import jax
import jax.numpy as jnp
from jax import lax
from jax.experimental import pallas as pl
from jax.experimental.pallas import tpu as pltpu

F32 = jnp.float32
BF16 = jnp.bfloat16

D_MODEL = 1024
DEPTH = 4
PAGE_SIZE = 128
D_INNER = 2048
SSM_HEADDIM = 64
SSM_HEADS = 32
SSM_GROUPS = 8
HEADS_PER_GROUP = 4
D_STATE = 128
CONV_W = 4
CONV_DIM = 4096
SSD_CHUNK = 128
MLA_HEADS = 16
QK_NOPE = 128
QK_ROPE = 64
V_DIM = 128
Q_LORA = 256
KV_LORA = 256
ROPE_BASE = 10000.0
N_EXPERTS = 8
ALPHA = (2.0 * DEPTH) ** 0.25
LN_EPS = 1e-5
RMS_EPS = 1e-6

LANES = 128
SUBLANES = 8
VMEM_LIMIT = 56 * 1024 * 1024

SM_QA = 0
SM_LAT = Q_LORA
SM_TAIL = Q_LORA + KV_LORA
SM_W = SM_TAIL + LANES
DT_OFF = QK_ROPE
NEG = -1e30


def _dot(a, b):
    return jnp.dot(a, b, preferred_element_type=F32)


def _dot_nt(a, b):
    return lax.dot_general(a, b, (((1,), (1,)), ((), ())), preferred_element_type=F32)


def _dot_tn(a, b):
    return lax.dot_general(a, b, (((0,), (0,)), ((), ())), preferred_element_type=F32)


def _silu(x):
    h = 0.5 * x
    return h + h * jnp.tanh(h)


def _softplus(x):
    return jnp.maximum(x, 0.0) + jnp.log1p(jnp.exp(-jnp.abs(x)))


def _params(*sem):
    return pltpu.CompilerParams(dimension_semantics=sem, vmem_limit_bytes=VMEM_LIMIT)


def _layernorm(r, g, b):
    mu = jnp.mean(r, axis=-1, keepdims=True)
    d = r - mu
    var = jnp.mean(d * d, axis=-1, keepdims=True)
    return d * lax.rsqrt(var + LN_EPS) * g + b


def _rms(x, g):
    return x * lax.rsqrt(jnp.mean(x * x, axis=-1, keepdims=True) + RMS_EPS) * g


def _mm_kernel(x_ref, w_ref, o_ref):
    o_ref[...] = _dot(x_ref[...].astype(BF16), w_ref[...]).astype(o_ref.dtype)


def _matmul(x, w, out_dtype, tm=1024, tn=1024):
    m, k = x.shape
    n = w.shape[1]
    tm = min(tm, m)
    tn = min(tn, n)
    return pl.pallas_call(
        _mm_kernel,
        grid=(m // tm, n // tn),
        in_specs=[pl.BlockSpec((tm, k), lambda i, j: (i, 0)),
                  pl.BlockSpec((k, tn), lambda i, j: (0, j))],
        out_specs=pl.BlockSpec((tm, tn), lambda i, j: (i, j)),
        out_shape=jax.ShapeDtypeStruct((m, n), out_dtype),
        compiler_params=_params("arbitrary", "arbitrary"),
        name="in_proj",
    )(x, w)


def _expand_heads(col_blk, lane, j):
    a = col_blk[:, DT_OFF + 2 * j:DT_OFF + 2 * j + 1]
    b = col_blk[:, DT_OFF + 2 * j + 1:DT_OFF + 2 * j + 2]
    return jnp.where(lane < SSM_HEADDIM, a, b)


def _ssd_kernel(xbc_ref, z_ref, sm_ref, cw_ref, cb_ref, dtb_ref, alog_ref, dexp_ref, ng_ref,
                y_ref, hout_ref, cout_ref, ext_ref, h_ref, ybuf_ref, *, L):
    c = pl.program_id(1)
    nc = pl.num_programs(1)
    rp = HEADS_PER_GROUP * SSM_HEADDIM
    gn = SSM_GROUPS * D_STATE
    hist = 2 * SUBLANES

    ub = xbc_ref[0]
    u = ub.astype(F32)

    @pl.when(c == 0)
    def _():
        ext_ref[0:hist, :] = jnp.zeros((hist, CONV_DIM), BF16)
        h_ref[...] = jnp.zeros_like(h_ref)

    @pl.when(c > 0)
    def _():
        ext_ref[0:hist, :] = ext_ref[L:L + hist, :]

    ext_ref[hist:hist + L, :] = ub
    srow = lax.broadcasted_iota(jnp.int32, ((CONV_W - 1) * L, hist + L), 0)
    scol = lax.broadcasted_iota(jnp.int32, ((CONV_W - 1) * L, hist + L), 1)
    blk = sum((srow >= s * L).astype(jnp.int32) for s in range(1, CONV_W - 1))
    sel = scol == srow + (hist - 1) - blk * (L + 1)
    shifted = _dot(sel.astype(BF16), ext_ref[...])
    acc = cb_ref[...] + u * cw_ref[CONV_W - 1:CONV_W, :]
    for s in range(1, CONV_W):
        acc = acc + shifted[(s - 1) * L:s * L, :] * cw_ref[CONV_W - 1 - s:CONV_W - s, :]
    act = _silu(acc)

    tail = sm_ref[0, :, SM_TAIL:SM_TAIL + LANES]
    dt = _softplus(tail + dtb_ref[...])
    da = dt * (-jnp.exp(alog_ref[...]))
    row = lax.broadcasted_iota(jnp.int32, (L, L), 0)
    col = lax.broadcasted_iota(jnp.int32, (L, L), 1)
    causal = row >= col
    tri = causal.astype(F32)
    acum = jnp.dot(tri, da, precision=lax.Precision.HIGHEST, preferred_element_type=F32) * math.log2(math.e)
    acum_t = acum.T
    a_last = acum[L - 1:L, :]
    to_end = jnp.exp2(a_last - acum)
    e_acum = jnp.exp2(acum)
    cdec = jnp.exp2(a_last)
    lane = lax.broadcasted_iota(jnp.int32, (L, LANES), 1)

    def expand_group(v, g):
        return jnp.concatenate([_expand_heads(v, lane, 2 * g), _expand_heads(v, lane, 2 * g + 1)], axis=1)

    for g in range(SSM_GROUPS):
        xs_g = act[:, g * rp:(g + 1) * rp]
        bb = act[:, D_INNER + g * D_STATE:D_INNER + (g + 1) * D_STATE].astype(BF16)
        cbf = act[:, D_INNER + gn + g * D_STATE:D_INNER + gn + (g + 1) * D_STATE].astype(BF16)
        xdt = xs_g * expand_group(dt, g)
        xdtb = xdt.astype(BF16)
        cb = _dot_nt(cbf, bb)
        yd = []
        for r in range(HEADS_PER_GROUP):
            hd = DT_OFF + g * HEADS_PER_GROUP + r
            seg = acum[:, hd:hd + 1] - acum_t[hd:hd + 1, :]
            dec = jnp.exp2(jnp.where(causal, seg, NEG))
            yd.append(_dot((cb * dec).astype(BF16), xdtb[:, r * SSM_HEADDIM:(r + 1) * SSM_HEADDIM]))
        y_diag = jnp.concatenate(yd, axis=1)
        hg = h_ref[g * rp:(g + 1) * rp, :]
        y_off = _dot_nt(cbf, hg.astype(BF16)) * expand_group(e_acum, g)
        xw = (xdt * expand_group(to_end, g)).astype(BF16)
        st = _dot_tn(xw, bb)
        for r in range(HEADS_PER_GROUP):
            hd = DT_OFF + g * HEADS_PER_GROUP + r
            lo = g * rp + r * SSM_HEADDIM
            h_ref[lo:lo + SSM_HEADDIM, :] = (cdec[0:1, hd:hd + 1] * hg[r * SSM_HEADDIM:(r + 1) * SSM_HEADDIM, :]
                                             + st[r * SSM_HEADDIM:(r + 1) * SSM_HEADDIM, :])
        ybuf_ref[:, g * rp:(g + 1) * rp] = y_diag + y_off + xs_g * dexp_ref[:, g * rp:(g + 1) * rp]

    yz = ybuf_ref[...] * _silu(z_ref[0].astype(F32))
    y_ref[0] = _rms(yz, ng_ref[...]).astype(y_ref.dtype)

    @pl.when(c == nc - 1)
    def _():
        hout_ref[0] = h_ref[...]
        cout_ref[0] = u[L - (CONV_W - 1):L, :]


def _ssd_prompt(big, small, lw):
    nb, t, _ = big.shape
    L = SSD_CHUNK if t % SSD_CHUNK == 0 else t
    nc = t // L
    kern = functools.partial(_ssd_kernel, L=L)
    const = lambda b, c: (0, 0)
    return pl.pallas_call(
        kern,
        grid=(nb, nc),
        in_specs=[
            pl.BlockSpec((1, L, CONV_DIM), lambda b, c: (b, c, 0)),
            pl.BlockSpec((1, L, D_INNER), lambda b, c: (b, c, 2)),
            pl.BlockSpec((1, L, SM_W), lambda b, c: (b, c, 0)),
            pl.BlockSpec((CONV_W, CONV_DIM), const),
            pl.BlockSpec((1, CONV_DIM), const),
            pl.BlockSpec((1, LANES), const),
            pl.BlockSpec((1, LANES), const),
            pl.BlockSpec((1, D_INNER), const),
            pl.BlockSpec((1, D_INNER), const),
        ],
        out_specs=[
            pl.BlockSpec((1, L, D_INNER), lambda b, c: (b, c, 0)),
            pl.BlockSpec((1, D_INNER, D_STATE), lambda b, c: (b, 0, 0)),
            pl.BlockSpec((1, CONV_W - 1, CONV_DIM), lambda b, c: (b, 0, 0)),
        ],
        out_shape=[
            jax.ShapeDtypeStruct((nb, t, D_INNER), BF16),
            jax.ShapeDtypeStruct((nb, D_INNER, D_STATE), F32),
            jax.ShapeDtypeStruct((nb, CONV_W - 1, CONV_DIM), F32),
        ],
        scratch_shapes=[
            pltpu.VMEM((L + 2 * SUBLANES, CONV_DIM), BF16),
            pltpu.VMEM((D_INNER, D_STATE), F32),
            pltpu.VMEM((L, D_INNER), F32),
        ],
        compiler_params=_params("arbitrary", "arbitrary"),
        name="ssd_prompt",
    )(big, big, small, lw["conv_w"], lw["conv_b"], lw["dt_bias"], lw["a_log"], lw["d_exp"], lw["ssm_norm_g"])


def _ssd_step_kernel(xbc_ref, z_ref, sm_ref, cs_ref, hs_ref, cw_ref, cb_ref, dtb_ref, alog_ref, dexp_ref, ng_ref,
                     y_ref, hout_ref, cout_ref):
    gn = SSM_GROUPS * D_STATE
    u = xbc_ref[0].astype(F32)
    prev = cs_ref[0]
    acc = cb_ref[...] + u * cw_ref[CONV_W - 1:CONV_W, :]
    for k in range(CONV_W - 1):
        acc = acc + prev[k:k + 1, :] * cw_ref[k:k + 1, :]
    cout_ref[0, 0:CONV_W - 2, :] = prev[1:CONV_W - 1, :]
    cout_ref[0, CONV_W - 2:CONV_W - 1, :] = u
    act = _silu(acc)

    tail = sm_ref[0, :, SM_TAIL:SM_TAIL + LANES]
    dt = _softplus(tail + dtb_ref[...])
    d_a = jnp.exp(dt * (-jnp.exp(alog_ref[...])))
    lane = lax.broadcasted_iota(jnp.int32, (1, LANES), 1)
    ys = []
    for j in range(D_INNER // LANES):
        g = (j * LANES) // (HEADS_PER_GROUP * SSM_HEADDIM)
        xs_j = act[:, j * LANES:(j + 1) * LANES]
        xdt = xs_j * _expand_heads(dt, lane, j)
        xcol = jnp.broadcast_to(xdt, (LANES, LANES)).T
        dcol = jnp.broadcast_to(_expand_heads(d_a, lane, j), (LANES, LANES)).T
        b_g = act[:, D_INNER + g * D_STATE:D_INNER + (g + 1) * D_STATE]
        c_g = act[:, D_INNER + gn + g * D_STATE:D_INNER + gn + (g + 1) * D_STATE]
        hnew = dcol * hs_ref[0, j * LANES:(j + 1) * LANES, :] + xcol * b_g
        hout_ref[0, j * LANES:(j + 1) * LANES, :] = hnew
        yrow = jnp.sum((hnew * c_g).T, axis=0, keepdims=True)
        ys.append(yrow + xs_j * dexp_ref[:, j * LANES:(j + 1) * LANES])
    y = jnp.concatenate(ys, axis=1)
    yz = y * _silu(z_ref[0].astype(F32))
    y_ref[0] = _rms(yz, ng_ref[...]).astype(y_ref.dtype)


def _ssd_step(layer, big, small, conv_state, ssm_state, lw):
    n = big.shape[0]
    const = lambda s: (0, 0)
    return pl.pallas_call(
        _ssd_step_kernel,
        grid=(n,),
        in_specs=[
            pl.BlockSpec((1, 1, CONV_DIM), lambda s: (s, 0, 0)),
            pl.BlockSpec((1, 1, D_INNER), lambda s: (s, 0, 2)),
            pl.BlockSpec((1, 1, SM_W), lambda s: (s, 0, 0)),
            pl.BlockSpec((1, CONV_W - 1, CONV_DIM), lambda s: (s, 0, 0)),
            pl.BlockSpec((None, 1, D_INNER, D_STATE), lambda s: (layer, s, 0, 0)),
            pl.BlockSpec((CONV_W, CONV_DIM), const),
            pl.BlockSpec((1, CONV_DIM), const),
            pl.BlockSpec((1, LANES), const),
            pl.BlockSpec((1, LANES), const),
            pl.BlockSpec((1, D_INNER), const),
            pl.BlockSpec((1, D_INNER), const),
        ],
        out_specs=[
            pl.BlockSpec((1, 1, D_INNER), lambda s: (s, 0, 0)),
            pl.BlockSpec((1, D_INNER, D_STATE), lambda s: (s, 0, 0)),
            pl.BlockSpec((1, CONV_W - 1, CONV_DIM), lambda s: (s, 0, 0)),
        ],
        out_shape=[
            jax.ShapeDtypeStruct((n, 1, D_INNER), BF16),
            jax.ShapeDtypeStruct((n, D_INNER, D_STATE), F32),
            jax.ShapeDtypeStruct((n, CONV_W - 1, CONV_DIM), F32),
        ],
        compiler_params=_params("arbitrary"),
        name="ssd_step",
    )(big, big, small, conv_state, ssm_state, lw["conv_w"], lw["conv_b"], lw["dt_bias"], lw["a_log"],
      lw["d_exp"], lw["ssm_norm_g"])


def _prep_kernel(sm_ref, cos_ref, sin_ref, qg_ref, kvg_ref, wqn_ref, wqr_ref, wkt_ref,
                 qlat_ref, qrope_ref, kc_ref, kr_ref, latf_ref, krf_ref, kct_ref, krt_ref, *, scale):
    sm = sm_ref[0]
    tm = sm.shape[0]
    half = QK_ROPE // 2
    qn = _rms(sm[:, SM_QA:SM_QA + Q_LORA], qg_ref[...]).astype(BF16)
    qnope = _dot(qn, wqn_ref[...])
    qr = _dot(qn, wqr_ref[...])
    cos = cos_ref[...]
    sin = sin_ref[...]
    wq = MLA_HEADS * QK_ROPE
    reps = wq // LANES
    cosq = jnp.concatenate([cos] * reps, axis=1)
    sinq = jnp.concatenate([sin] * reps, axis=1)
    lane_q = lax.broadcasted_iota(jnp.int32, (tm, wq), 1)
    swap = jnp.where((lane_q & (QK_ROPE - 1)) < half, pltpu.roll(qr, wq - half, 1), pltpu.roll(qr, half, 1))
    qrot = (qr * cosq + swap * sinq) * scale
    for h in range(MLA_HEADS):
        qrope_ref[0, h] = qrot[:, h * QK_ROPE:(h + 1) * QK_ROPE].astype(BF16)
        ql = _dot(qnope[:, h * QK_NOPE:(h + 1) * QK_NOPE].astype(BF16), wkt_ref[h]) * scale
        qlat_ref[0, h] = ql.astype(BF16)
    cn = _rms(sm[:, SM_LAT:SM_LAT + KV_LORA], kvg_ref[...])
    latf_ref[0] = cn
    kc_ref[0] = cn.astype(BF16)
    kct_ref[0] = cn.T.astype(BF16)
    tail = sm[:, SM_TAIL:SM_TAIL + LANES]
    lane_k = lax.broadcasted_iota(jnp.int32, (tm, LANES), 1)
    swapk = jnp.where(lane_k < half, pltpu.roll(tail, LANES - half, 1), pltpu.roll(tail, half, 1))
    krot = tail * cos + swapk * sin
    krf_ref[0] = krot[:, 0:QK_ROPE]
    kr_ref[0] = krot[:, 0:QK_ROPE].astype(BF16)
    krt_ref[0] = krot.T[0:QK_ROPE, :].astype(BF16)


def _mla_prep(small, cos, sin, lw, tm=256):
    nb, t, _ = small.shape
    tm = min(tm, t)
    scale = (QK_NOPE + QK_ROPE) ** -0.5 * math.log2(math.e)
    const2 = lambda b, i: (0, 0)
    return pl.pallas_call(
        functools.partial(_prep_kernel, scale=scale),
        grid=(nb, t // tm),
        in_specs=[
            pl.BlockSpec((1, tm, SM_W), lambda b, i: (b, i, 0)),
            pl.BlockSpec((tm, LANES), lambda b, i: (i, 0)),
            pl.BlockSpec((tm, LANES), lambda b, i: (i, 0)),
            pl.BlockSpec((1, Q_LORA), const2),
            pl.BlockSpec((1, KV_LORA), const2),
            pl.BlockSpec((Q_LORA, MLA_HEADS * QK_NOPE), const2),
            pl.BlockSpec((Q_LORA, MLA_HEADS * QK_ROPE), const2),
            pl.BlockSpec((MLA_HEADS, QK_NOPE, KV_LORA), lambda b, i: (0, 0, 0)),
        ],
        out_specs=[
            pl.BlockSpec((1, MLA_HEADS, tm, KV_LORA), lambda b, i: (b, 0, i, 0)),
            pl.BlockSpec((1, MLA_HEADS, tm, QK_ROPE), lambda b, i: (b, 0, i, 0)),
            pl.BlockSpec((1, tm, KV_LORA), lambda b, i: (b, i, 0)),
            pl.BlockSpec((1, tm, QK_ROPE), lambda b, i: (b, i, 0)),
            pl.BlockSpec((1, tm, KV_LORA), lambda b, i: (b, i, 0)),
            pl.BlockSpec((1, tm, QK_ROPE), lambda b, i: (b, i, 0)),
            pl.BlockSpec((1, KV_LORA, tm), lambda b, i: (b, 0, i)),
            pl.BlockSpec((1, QK_ROPE, tm), lambda b, i: (b, 0, i)),
        ],
        out_shape=[
            jax.ShapeDtypeStruct((nb, MLA_HEADS, t, KV_LORA), BF16),
            jax.ShapeDtypeStruct((nb, MLA_HEADS, t, QK_ROPE), BF16),
            jax.ShapeDtypeStruct((nb, t, KV_LORA), BF16),
            jax.ShapeDtypeStruct((nb, t, QK_ROPE), BF16),
            jax.ShapeDtypeStruct((nb, t, KV_LORA), F32),
            jax.ShapeDtypeStruct((nb, t, QK_ROPE), F32),
            jax.ShapeDtypeStruct((nb, KV_LORA, t), BF16),
            jax.ShapeDtypeStruct((nb, QK_ROPE, t), BF16),
        ],
        compiler_params=_params("arbitrary", "arbitrary"),
        name="mla_prep",
    )(small, cos, sin, lw["q_norm_g"], lw["kv_norm_g"], lw["w_q_nope"], lw["w_q_rope"], lw["w_k_t"])


def _lane_tiles(x):
    return [x[:, k * LANES:(k + 1) * LANES] for k in range(x.shape[1] // LANES)]


def _softmax_step(s, m_old=None, l_old=None):
    tiles = _lane_tiles(s)
    rows = s.shape[0]
    row_max = jnp.broadcast_to(jnp.max(functools.reduce(jnp.maximum, tiles), axis=1, keepdims=True), (rows, LANES))
    m_new = row_max if m_old is None else jnp.maximum(m_old, row_max)
    p_tiles = [jnp.exp2(t - m_new) for t in tiles]
    l_new = jnp.broadcast_to(jnp.sum(functools.reduce(jnp.add, p_tiles), axis=1, keepdims=True), (rows, LANES))
    alpha = None
    if m_old is not None:
        alpha = jnp.exp2(m_old - m_new)
        l_new = alpha * l_old + l_new
    return m_new, alpha, l_new, jnp.concatenate(p_tiles, axis=1)


def _attn_kernel(qlat_ref, qrope_ref, kc_ref, kct_ref, krt_ref, wv_ref, o_ref, m_ref, l_ref, acc_ref, *, tq, tk, hc):
    i = pl.program_id(1)
    cr = hc * tq
    reps = KV_LORA // LANES

    def block(j, diagonal):
        start = pl.multiple_of(j * tk, tk)
        kc = kc_ref[0, pl.ds(start, tk), :]
        kct = kct_ref[0, :, pl.ds(start, tk)]
        krt = krt_ref[0, :, pl.ds(start, tk)]
        if diagonal:
            qpos = i * tq + lax.broadcasted_iota(jnp.int32, (hc, tq, tk), 1).reshape(cr, tk)
            kpos = j * tk + lax.broadcasted_iota(jnp.int32, (cr, tk), 1)
            visible = kpos <= qpos
        for c in range(MLA_HEADS // hc):
            r0 = c * cr
            q1 = qlat_ref[0, c * hc:(c + 1) * hc].reshape(cr, KV_LORA)
            q2 = qrope_ref[0, c * hc:(c + 1) * hc].reshape(cr, QK_ROPE)
            s = _dot(q1, kct) + _dot(q2, krt)
            if diagonal:
                s = jnp.where(visible, s, NEG)
                m_old = l_old = None
            else:
                m_old = m_ref[r0:r0 + cr, :]
                l_old = l_ref[r0:r0 + cr, :]
            m_new, alpha, l_new, p = _softmax_step(s, m_old, l_old)
            l_ref[r0:r0 + cr, :] = l_new
            m_ref[r0:r0 + cr, :] = m_new
            pv = _dot(p.astype(BF16), kc)
            if diagonal:
                acc_ref[r0:r0 + cr, :] = pv
            else:
                acc_ref[r0:r0 + cr, :] = jnp.concatenate([alpha] * reps, axis=1) * acc_ref[r0:r0 + cr, :] + pv

    nfull = (i * tq) // tk
    block(nfull, True)

    def body(j, carry):
        block(j, False)
        return carry

    lax.fori_loop(0, nfull, body, 0)
    for h in range(MLA_HEADS):
        l_h = jnp.concatenate([l_ref[h * tq:(h + 1) * tq, :]] * reps, axis=1)
        o_lat = (acc_ref[h * tq:(h + 1) * tq, :] / l_h).astype(BF16)
        o_ref[0, :, h * V_DIM:(h + 1) * V_DIM] = _dot(o_lat, wv_ref[h]).astype(o_ref.dtype)


def _attn_prompt(qlat, qrope, kc, kct, krt, wv, tq=128, tk=256, hc=16):
    nb, _, t, _ = qlat.shape
    tk = min(tk, t)
    tq = min(tq, tk)
    rows = MLA_HEADS * tq
    return pl.pallas_call(
        functools.partial(_attn_kernel, tq=tq, tk=tk, hc=hc),
        grid=(nb, t // tq),
        in_specs=[
            pl.BlockSpec((1, MLA_HEADS, tq, KV_LORA), lambda b, i: (b, 0, i, 0)),
            pl.BlockSpec((1, MLA_HEADS, tq, QK_ROPE), lambda b, i: (b, 0, i, 0)),
            pl.BlockSpec((1, t, KV_LORA), lambda b, i: (b, 0, 0)),
            pl.BlockSpec((1, KV_LORA, t), lambda b, i: (b, 0, 0)),
            pl.BlockSpec((1, QK_ROPE, t), lambda b, i: (b, 0, 0)),
            pl.BlockSpec((MLA_HEADS, KV_LORA, V_DIM), lambda b, i: (0, 0, 0)),
        ],
        out_specs=pl.BlockSpec((1, tq, MLA_HEADS * V_DIM), lambda b, i: (b, i, 0)),
        out_shape=jax.ShapeDtypeStruct((nb, t, MLA_HEADS * V_DIM), BF16),
        scratch_shapes=[pltpu.VMEM((rows, LANES), F32), pltpu.VMEM((rows, LANES), F32),
                        pltpu.VMEM((rows, KV_LORA), F32)],
        compiler_params=_params("arbitrary", "arbitrary"),
        name="attn_prompt",
    )(qlat, qrope, kc, kct, krt, wv)


def _decode_kernel(pt_ref, qlat_ref, qrope_ref, kcn_ref, krn_ref, lat_hbm, krt_hbm, o_ref,
                   lat_buf, krt_buf, s_buf, sem, *, layer, npages, chunk_pages):
    s = pl.program_id(0)
    n = pl.num_programs(0)
    slot = lax.rem(s, 2)
    ck = chunk_pages * PAGE_SIZE
    nchunk = npages // chunk_pages

    def page_copies(seq, slot_, p):
        pg = pt_ref[seq * npages + p]
        keys = pl.ds(p * PAGE_SIZE, PAGE_SIZE)
        return (pltpu.make_async_copy(lat_hbm.at[layer, pg], lat_buf.at[slot_, keys, :], sem.at[0, slot_]),
                pltpu.make_async_copy(krt_hbm.at[layer, pg], krt_buf.at[slot_, :, keys], sem.at[1, slot_]))

    def start_all(seq, slot_):
        for p in range(npages):
            for cp in page_copies(seq, slot_, p):
                cp.start()

    @pl.when(s == 0)
    def _():
        start_all(0, 0)

    @pl.when(s + 1 < n)
    def _():
        start_all(s + 1, 1 - slot)

    for p in range(npages):
        for cp in page_copies(s, slot, p):
            cp.wait()

    q1 = qlat_ref[0]
    q2 = qrope_ref[0]
    kn = kcn_ref[0].astype(F32)
    s_new = (jnp.sum(q1.astype(F32) * kn, axis=1, keepdims=True)
             + jnp.sum(q2.astype(F32) * krn_ref[0].astype(F32), axis=1, keepdims=True))

    def chunk_keys(c):
        lat = lat_buf[slot, c * ck:(c + 1) * ck, :].astype(BF16)
        krt = krt_buf[slot, :, c * ck:(c + 1) * ck].astype(BF16)
        return lat, krt

    mx = jnp.broadcast_to(s_new, (MLA_HEADS, LANES))
    for c in range(nchunk):
        lat, krt = chunk_keys(c)
        sc = _dot_nt(q1, lat) + _dot(q2, krt)
        s_buf[:, c * ck:(c + 1) * ck] = sc
        mx = jnp.maximum(mx, functools.reduce(jnp.maximum, _lane_tiles(sc)))
    m = jnp.max(mx, axis=1, keepdims=True)
    p_new = jnp.exp2(s_new - m)
    l = p_new
    acc = p_new * kn
    for c in range(nchunk):
        lat, _ = chunk_keys(c)
        p = jnp.exp2(s_buf[:, c * ck:(c + 1) * ck] - m)
        l = l + jnp.sum(p, axis=1, keepdims=True)
        acc = acc + _dot(p.astype(BF16), lat)
    o_ref[0] = (acc / l).astype(o_ref.dtype)


def _attn_decode(layer, page_table, cache_latent, cache_krope_t, qlat, qrope, kcn, krn, chunk_pages=8):
    n, npages = page_table.shape
    chunk_pages = math.gcd(chunk_pages, npages)
    n_keys = npages * PAGE_SIZE
    pt = page_table.reshape(-1)
    row = lambda s, pt_ref: (s, 0, 0)
    grid_spec = pltpu.PrefetchScalarGridSpec(
        num_scalar_prefetch=1,
        grid=(n,),
        in_specs=[
            pl.BlockSpec((1, MLA_HEADS, KV_LORA), row),
            pl.BlockSpec((1, MLA_HEADS, QK_ROPE), row),
            pl.BlockSpec((1, 1, KV_LORA), row),
            pl.BlockSpec((1, 1, QK_ROPE), row),
            pl.BlockSpec(memory_space=pl.ANY),
            pl.BlockSpec(memory_space=pl.ANY),
        ],
        out_specs=pl.BlockSpec((1, MLA_HEADS, KV_LORA), row),
        scratch_shapes=[
            pltpu.VMEM((2, n_keys, KV_LORA), F32),
            pltpu.VMEM((2, QK_ROPE, n_keys), F32),
            pltpu.VMEM((MLA_HEADS, n_keys), F32),
            pltpu.SemaphoreType.DMA((2, 2)),
        ],
    )
    return pl.pallas_call(
        functools.partial(_decode_kernel, layer=layer, npages=npages, chunk_pages=chunk_pages),
        grid_spec=grid_spec,
        out_shape=jax.ShapeDtypeStruct((n, MLA_HEADS, KV_LORA), BF16),
        compiler_params=_params("arbitrary"),
        name="attn_decode",
    )(pt, qlat, qrope, kcn, krn, cache_latent, cache_krope_t)


def _vproj_kernel(o_ref, wv_ref, y_ref):
    for h in range(MLA_HEADS):
        y_ref[:, h * V_DIM:(h + 1) * V_DIM] = _dot(o_ref[h], wv_ref[h]).astype(y_ref.dtype)


def _vproj(o_lat, wv):
    n = o_lat.shape[1]
    return pl.pallas_call(
        _vproj_kernel,
        out_shape=jax.ShapeDtypeStruct((n, MLA_HEADS * V_DIM), BF16),
        name="v_proj",
    )(o_lat, wv)


def _merge_kernel(ys_ref, ym_ref, g_ref, x_ref, wbs_ref, wbm_ref, wo_ref, lg_ref, lb_ref, o_ref):
    gates = g_ref[...].astype(F32)
    a = _dot(ys_ref[...], wbs_ref[...])
    b = _dot(ym_ref[...], wbm_ref[...])
    m = jax.nn.sigmoid(gates[:, 0:D_MODEL]) * a + jax.nn.sigmoid(gates[:, D_MODEL:2 * D_MODEL]) * b
    r = ALPHA * x_ref[...] + _dot(m.astype(BF16), wo_ref[...])
    o_ref[...] = _layernorm(r, lg_ref[...], lb_ref[...])


def _merge(y_ssm, y_mla, big, x, lw, tm=512):
    m = x.shape[0]
    tm = min(tm, m)
    const = lambda i: (0, 0)
    return pl.pallas_call(
        _merge_kernel,
        grid=(m // tm,),
        in_specs=[
            pl.BlockSpec((tm, D_INNER), lambda i: (i, 0)),
            pl.BlockSpec((tm, MLA_HEADS * V_DIM), lambda i: (i, 0)),
            pl.BlockSpec((tm, 2 * D_MODEL), lambda i: (i, 3)),
            pl.BlockSpec((tm, D_MODEL), lambda i: (i, 0)),
            pl.BlockSpec((D_INNER, D_MODEL), const),
            pl.BlockSpec((MLA_HEADS * V_DIM, D_MODEL), const),
            pl.BlockSpec((D_MODEL, D_MODEL), const),
            pl.BlockSpec((1, D_MODEL), const),
            pl.BlockSpec((1, D_MODEL), const),
        ],
        out_specs=pl.BlockSpec((tm, D_MODEL), lambda i: (i, 0)),
        out_shape=jax.ShapeDtypeStruct((m, D_MODEL), F32),
        compiler_params=_params("arbitrary"),
        name="merge_ln",
    )(y_ssm, y_mla, big, x, lw["w_br_ssm"], lw["w_br_mla"], lw["w_out"], lw["ln1_g"], lw["ln1_b"])


def _swiglu_hidden(xb, wgu):
    tf = wgu.shape[1] // 2
    gu = _dot(xb, wgu)
    return (_silu(gu[:, 0:tf]) * gu[:, tf:2 * tf]).astype(BF16)


def _fuse_gate_up(wg, wu, tf):
    lead = wg.shape[:-1]
    nj = wg.shape[-1] // tf
    both = jnp.stack([wg.reshape(lead + (nj, tf)), wu.reshape(lead + (nj, tf))], axis=-2)
    return both.reshape(lead + (2 * nj * tf,))


def _ffn_kernel(x_ref, wgu_ref, wd_ref, lg_ref, lb_ref, o_ref, acc_ref):
    j = pl.program_id(1)
    h = _swiglu_hidden(x_ref[...].astype(BF16), wgu_ref[...])
    part = _dot(h, wd_ref[...])

    @pl.when(j == 0)
    def _():
        acc_ref[...] = part

    @pl.when(j > 0)
    def _():
        acc_ref[...] += part

    @pl.when(j == pl.num_programs(1) - 1)
    def _():
        o_ref[...] = _layernorm(ALPHA * x_ref[...] + acc_ref[...], lg_ref[...], lb_ref[...])


def _ff_tile(d_ff):
    half = d_ff // 2
    return half if half % LANES == 0 else d_ff


def _ffn(x, wgu, wd, ln_g, ln_b, tm=512):
    m = x.shape[0]
    d_ff = wd.shape[0]
    tm = min(tm, m)
    tf = _ff_tile(d_ff)
    const = lambda i, j: (0, 0)
    return pl.pallas_call(
        _ffn_kernel,
        grid=(m // tm, d_ff // tf),
        in_specs=[
            pl.BlockSpec((tm, D_MODEL), lambda i, j: (i, 0)),
            pl.BlockSpec((D_MODEL, 2 * tf), lambda i, j: (0, j)),
            pl.BlockSpec((tf, D_MODEL), lambda i, j: (j, 0)),
            pl.BlockSpec((1, D_MODEL), const),
            pl.BlockSpec((1, D_MODEL), const),
        ],
        out_specs=pl.BlockSpec((tm, D_MODEL), lambda i, j: (i, 0)),
        out_shape=jax.ShapeDtypeStruct((m, D_MODEL), F32),
        scratch_shapes=[pltpu.VMEM((tm, D_MODEL), F32)],
        compiler_params=_params("arbitrary", "arbitrary"),
        name="ffn_ln",
    )(x, wgu, wd, ln_g, ln_b)


def _router_kernel(x_ref, rw_ref, c_ref):
    logits = jnp.dot(x_ref[...], rw_ref[...], precision=lax.Precision.HIGHEST, preferred_element_type=F32)
    lane = lax.broadcasted_iota(jnp.int32, logits.shape, 1)
    logits = jnp.where(lane < N_EXPERTS, logits, NEG)
    m1 = jnp.max(logits, axis=1, keepdims=True)
    i1 = jnp.min(jnp.where(logits == m1, lane, LANES), axis=1, keepdims=True)
    rest = jnp.where(lane == i1, NEG, logits)
    m2 = jnp.max(rest, axis=1, keepdims=True)
    i2 = jnp.min(jnp.where(rest == m2, lane, LANES), axis=1, keepdims=True)
    e2 = jnp.exp(m2 - m1)
    den = 1.0 + e2
    c_ref[...] = jnp.where(lane == i1, 1.0 / den, 0.0) + jnp.where(lane == i2, e2 / den, 0.0)


def _router(x, rw_pad, tm=512):
    m = x.shape[0]
    tm = min(tm, m)
    return pl.pallas_call(
        _router_kernel,
        grid=(m // tm,),
        in_specs=[pl.BlockSpec((tm, D_MODEL), lambda i: (i, 0)),
                  pl.BlockSpec((D_MODEL, LANES), lambda i: (0, 0))],
        out_specs=pl.BlockSpec((tm, LANES), lambda i: (i, 0)),
        out_shape=jax.ShapeDtypeStruct((m, LANES), F32),
        compiler_params=_params("arbitrary"),
        name="router",
    )(x, rw_pad)


def _moe_kernel(x_ref, c_ref, wgu_ref, wd_ref, lg_ref, lb_ref, o_ref, acc_ref):
    e = pl.program_id(1)
    j = pl.program_id(2)
    h = _swiglu_hidden(x_ref[...].astype(BF16), wgu_ref[0])
    comb = c_ref[...]
    lane = lax.broadcasted_iota(jnp.int32, comb.shape, 1)
    w_e = jnp.sum(jnp.where(lane == e, comb, 0.0), axis=1, keepdims=True)
    part = w_e * _dot(h, wd_ref[0])
    first = jnp.logical_and(e == 0, j == 0)

    @pl.when(first)
    def _():
        acc_ref[...] = part

    @pl.when(jnp.logical_not(first))
    def _():
        acc_ref[...] += part

    @pl.when(jnp.logical_and(e == pl.num_programs(1) - 1, j == pl.num_programs(2) - 1))
    def _():
        o_ref[...] = _layernorm(ALPHA * x_ref[...] + acc_ref[...], lg_ref[...], lb_ref[...])


def _moe(x, comb, wgu, wd, ln_g, ln_b, tm=512):
    m = x.shape[0]
    n_e, d_ff, _ = wd.shape
    tm = min(tm, m)
    tf = _ff_tile(d_ff)
    const = lambda i, e, j: (0, 0)
    return pl.pallas_call(
        _moe_kernel,
        grid=(m // tm, n_e, d_ff // tf),
        in_specs=[
            pl.BlockSpec((tm, D_MODEL), lambda i, e, j: (i, 0)),
            pl.BlockSpec((tm, LANES), lambda i, e, j: (i, 0)),
            pl.BlockSpec((1, D_MODEL, 2 * tf), lambda i, e, j: (e, 0, j)),
            pl.BlockSpec((1, tf, D_MODEL), lambda i, e, j: (e, j, 0)),
            pl.BlockSpec((1, D_MODEL), const),
            pl.BlockSpec((1, D_MODEL), const),
        ],
        out_specs=pl.BlockSpec((tm, D_MODEL), lambda i, e, j: (i, 0)),
        out_shape=jax.ShapeDtypeStruct((m, D_MODEL), F32),
        scratch_shapes=[pltpu.VMEM((tm, D_MODEL), F32)],
        compiler_params=_params("arbitrary", "arbitrary", "arbitrary"),
        name="moe_ln",
    )(x, comb, wgu, wd, ln_g, ln_b)


def _layer_weights(l, w_in, conv_w, conv_b, dt_bias, a_log, d_skip, ssm_norm_g, q_norm_g, w_qb, kv_norm_g,
                   w_kvb, w_br_ssm, w_br_mla, w_out, ln1_g, ln1_b, ln2_g, ln2_b):
    wi = w_in[l]
    o_xbc = D_INNER
    o_dt = o_xbc + CONV_DIM
    o_qa = o_dt + SSM_HEADS
    o_kv = o_qa + Q_LORA
    o_gs = o_kv + KV_LORA + QK_ROPE
    o_gm = o_gs + D_MODEL
    w_big = jnp.concatenate([wi[:, o_xbc:o_dt], wi[:, 0:o_xbc], wi[:, o_gs:o_gm], wi[:, o_gm:]], axis=1).astype(BF16)
    w_small = jnp.concatenate(
        [wi[:, o_qa:o_kv], wi[:, o_kv:o_gs], wi[:, o_dt:o_qa],
         jnp.zeros((D_MODEL, LANES - QK_ROPE - SSM_HEADS), F32)], axis=1).astype(BF16)

    def tail_pad(v):
        return jnp.zeros((1, LANES), F32).at[0, DT_OFF:DT_OFF + SSM_HEADS].set(v)

    wq = w_qb[l].reshape(Q_LORA, MLA_HEADS, QK_NOPE + QK_ROPE)
    wkv = w_kvb[l].reshape(KV_LORA, MLA_HEADS, QK_NOPE + V_DIM)
    return {
        "w_big": w_big,
        "w_small": w_small,
        "conv_w": conv_w[l],
        "conv_b": conv_b[l].reshape(1, CONV_DIM),
        "dt_bias": tail_pad(dt_bias[l]),
        "a_log": tail_pad(a_log[l]),
        "d_exp": jnp.repeat(d_skip[l], SSM_HEADDIM).reshape(1, D_INNER),
        "ssm_norm_g": ssm_norm_g[l].reshape(1, D_INNER),
        "q_norm_g": q_norm_g[l].reshape(1, Q_LORA),
        "kv_norm_g": kv_norm_g[l].reshape(1, KV_LORA),
        "w_q_nope": wq[:, :, :QK_NOPE].reshape(Q_LORA, MLA_HEADS * QK_NOPE).astype(BF16),
        "w_q_rope": wq[:, :, QK_NOPE:].reshape(Q_LORA, MLA_HEADS * QK_ROPE).astype(BF16),
        "w_k_t": jnp.transpose(wkv[:, :, :QK_NOPE], (1, 2, 0)).astype(BF16),
        "w_v": jnp.transpose(wkv[:, :, QK_NOPE:], (1, 0, 2)).astype(BF16),
        "w_br_ssm": w_br_ssm[l].astype(BF16),
        "w_br_mla": w_br_mla[l].astype(BF16),
        "w_out": w_out[l].astype(BF16),
        "ln1_g": ln1_g[l].reshape(1, D_MODEL),
        "ln1_b": ln1_b[l].reshape(1, D_MODEL),
        "ln2_g": ln2_g[l].reshape(1, D_MODEL),
        "ln2_b": ln2_b[l].reshape(1, D_MODEL),
    }


def _rope_tables(pos):
    half = QK_ROPE // 2
    inv = ROPE_BASE ** (-jnp.arange(0, QK_ROPE, 2, dtype=F32) / QK_ROPE)
    ang = pos.astype(F32)[:, None] * inv[None, :]
    cos, sin = jnp.cos(ang), jnp.sin(ang)
    cos_t = jnp.tile(cos, (1, LANES // half))
    sin_t = jnp.tile(jnp.concatenate([-sin, sin], axis=1), (1, LANES // QK_ROPE))
    return cos_t, sin_t


def kernel(x_prompt, x_sample, cache_latent, cache_krope, state_ssm, state_conv, page_table, w_in, conv_w, conv_b, dt_bias, a_log, d_skip, ssm_norm_g, q_norm_g, w_qb, kv_norm_g, w_kvb, w_br_ssm, w_br_mla, w_out, ln1_g, ln1_b, ln2_g, ln2_b, ffn_w_gate, ffn_w_up, ffn_w_down, router_w, moe_w_gate, moe_w_up, moe_w_down):
    nb, t, _ = x_prompt.shape
    ns = x_sample.shape[0]
    assert x_sample.shape[1] == 1
    n_past = page_table.shape[1] * PAGE_SIZE
    cos_p, sin_p = _rope_tables(jnp.arange(t, dtype=jnp.int32))
    cos_s, sin_s = _rope_tables(jnp.full((ns,), n_past, dtype=jnp.int32))

    cache_krope_t = jnp.swapaxes(cache_krope, 2, 3)
    state_ssm_r = state_ssm.reshape(DEPTH, ns, D_INNER, D_STATE)

    xp = x_prompt.reshape(nb * t, D_MODEL)
    xs = x_sample.reshape(ns, D_MODEL)
    outs = [[] for _ in range(8)]
    for l in range(DEPTH):
        lw = _layer_weights(l, w_in, conv_w, conv_b, dt_bias, a_log, d_skip, ssm_norm_g, q_norm_g, w_qb,
                            kv_norm_g, w_kvb, w_br_ssm, w_br_mla, w_out, ln1_g, ln1_b, ln2_g, ln2_b)
        big_p = _matmul(xp, lw["w_big"], BF16)
        small_p = _matmul(xp, lw["w_small"], F32)
        big_p3 = big_p.reshape(nb, t, -1)
        small_p3 = small_p.reshape(nb, t, SM_W)
        y_ssm_p, h_p, cv_p = _ssd_prompt(big_p3, small_p3, lw)
        qlat, qrope, kc, _, lat_f, kr_f, kct, krt = _mla_prep(small_p3, cos_p, sin_p, lw)
        y_mla_p = _attn_prompt(qlat, qrope, kc, kct, krt, lw["w_v"])
        xp = _merge(y_ssm_p.reshape(nb * t, D_INNER), y_mla_p.reshape(nb * t, -1), big_p, xp, lw)
        big_s = _matmul(xs, lw["w_big"], F32)
        small_s = _matmul(xs, lw["w_small"], F32)
        y_ssm_s, h_s, cv_s = _ssd_step(l, big_s.reshape(ns, 1, -1), small_s.reshape(ns, 1, SM_W),
                                       state_conv[l], state_ssm_r, lw)
        qlat_s, qrope_s, kc_s, kr_s, lat_fs, kr_fs, _, _ = _mla_prep(small_s.reshape(1, ns, SM_W), cos_s, sin_s, lw)
        o_lat = _attn_decode(l, page_table, cache_latent, cache_krope_t,
                             jnp.transpose(qlat_s[0], (1, 0, 2)), jnp.transpose(qrope_s[0], (1, 0, 2)),
                             kc_s.reshape(ns, 1, KV_LORA), kr_s.reshape(ns, 1, QK_ROPE))
        y_mla_s = _vproj(jnp.transpose(o_lat, (1, 0, 2)), lw["w_v"])
        xs = _merge(y_ssm_s.reshape(ns, D_INNER), y_mla_s, big_s, xs, lw)
        i = l // 2
        if l % 2 == 0:
            tf = _ff_tile(ffn_w_gate.shape[-1])
            wgu = _fuse_gate_up(ffn_w_gate[i].astype(BF16), ffn_w_up[i].astype(BF16), tf)
            wd = ffn_w_down[i].astype(BF16)
            xp = _ffn(xp, wgu, wd, lw["ln2_g"], lw["ln2_b"])
            xs = _ffn(xs, wgu, wd, lw["ln2_g"], lw["ln2_b"])
        else:
            tf = _ff_tile(moe_w_gate.shape[-1])
            wgu = _fuse_gate_up(moe_w_gate[i].astype(BF16), moe_w_up[i].astype(BF16), tf)
            wd = moe_w_down[i].astype(BF16)
            rw = jnp.zeros((D_MODEL, LANES), F32).at[:, :N_EXPERTS].set(router_w[i])
            xp = _moe(xp, _router(xp, rw), wgu, wd, lw["ln2_g"], lw["ln2_b"])
            xs = _moe(xs, _router(xs, rw), wgu, wd, lw["ln2_g"], lw["ln2_b"])
        for k, v in enumerate((lat_f, kr_f, h_p.reshape(nb, SSM_HEADS, SSM_HEADDIM, D_STATE), cv_p,
                               lat_fs.reshape(ns, 1, KV_LORA), kr_fs.reshape(ns, 1, QK_ROPE),
                               h_s.reshape(ns, SSM_HEADS, SSM_HEADDIM, D_STATE), cv_s)):
            outs[k].append(v)

    return (xp.reshape(nb, t, D_MODEL), xs.reshape(ns, 1, D_MODEL)) + tuple(jnp.stack(o) for o in outs)
```

```python
import functools
import math

import jax
import jax.numpy as jnp
from jax import lax
from jax.experimental import pallas as pl
from jax.experimental.pallas import tpu as pltpu

F32 = jnp.float32
BF16 = jnp.bfloat16

D_MODEL = 1024
DEPTH = 4
PAGE_SIZE = 128
D_INNER = 2048
SSM_HEADDIM = 64
SSM_HEADS = 32
SSM_GROUPS = 8
HEADS_PER_GROUP = 4
D_STATE = 128
CONV_W = 4
CONV_DIM = 4096
SSD_CHUNK = 128
MLA_HEADS = 16
QK_NOPE = 128
QK_ROPE = 64
V_DIM = 128
Q_LORA = 256
KV_LORA = 256
ROPE_BASE = 10000.0
N_EXPERTS = 8
ALPHA = (2.0 * DEPTH) ** 0.25
LN_EPS = 1e-5
RMS_EPS = 1e-6

LANES = 128
SUBLANES = 8
VMEM_LIMIT = 56 * 1024 * 1024

SM_QA = 0
SM_LAT = Q_LORA
SM_TAIL = Q_LORA + KV_LORA
SM_W = SM_TAIL + LANES
DT_OFF = QK_ROPE
NEG = -1e30


def _dot(a, b):
    return jnp.dot(a, b, preferred_element_type=F32)


def _dot_nt(a, b):
    return lax.dot_general(a, b, (((1,), (1,)), ((), ())), preferred_element_type=F32)


def _dot_tn(a, b):
    return lax.dot_general(a, b, (((0,), (0,)), ((), ())), preferred_element_type=F32)


def _silu(x):
    h = 0.5 * x
    return h + h * jnp.tanh(h)


def _softplus(x):
    return jnp.maximum(x, 0.0) + jnp.log1p(jnp.exp(-jnp.abs(x)))


def _params(*sem):
    return pltpu.CompilerParams(dimension_semantics=sem, vmem_limit_bytes=VMEM_LIMIT)


def _layernorm(r, g, b):
    mu = jnp.mean(r, axis=-1, keepdims=True)
    d = r - mu
    var = jnp.mean(d * d, axis=-1, keepdims=True)
    return d * lax.rsqrt(var + LN_EPS) * g + b


def _rms(x, g):
    return x * lax.rsqrt(jnp.mean(x * x, axis=-1, keepdims=True) + RMS_EPS) * g


def _mm_kernel(x_ref, w_ref, o_ref):
    o_ref[...] = _dot(x_ref[...].astype(BF16), w_ref[...]).astype(o_ref.dtype)


def _matmul(x, w, out_dtype, tm=1024, tn=1024):
    m, k = x.shape
    n = w.shape[1]
    tm = min(tm, m)
    tn = min(tn, n)
    return pl.pallas_call(
        _mm_kernel,
        grid=(m // tm, n // tn),
        in_specs=[pl.BlockSpec((tm, k), lambda i, j: (i, 0)),
                  pl.BlockSpec((k, tn), lambda i, j: (0, j))],
        out_specs=pl.BlockSpec((tm, tn), lambda i, j: (i, j)),
        out_shape=jax.ShapeDtypeStruct((m, n), out_dtype),
        compiler_params=_params("arbitrary", "arbitrary"),
        name="in_proj",
    )(x, w)


def _expand_heads(col_blk, lane, j):
    a = col_blk[:, DT_OFF + 2 * j:DT_OFF + 2 * j + 1]
    b = col_blk[:, DT_OFF + 2 * j + 1:DT_OFF + 2 * j + 2]
    return jnp.where(lane < SSM_HEADDIM, a, b)


def _ssd_kernel(xbc_ref, z_ref, sm_ref, cw_ref, cb_ref, dtb_ref, alog_ref, dexp_ref, ng_ref,
                y_ref, hout_ref, cout_ref, ext_ref, h_ref, ybuf_ref, *, L):
    c = pl.program_id(1)
    nc = pl.num_programs(1)
    rp = HEADS_PER_GROUP * SSM_HEADDIM
    gn = SSM_GROUPS * D_STATE
    hist = 2 * SUBLANES

    ub = xbc_ref[0]
    u = ub.astype(F32)

    @pl.when(c == 0)
    def _():
        ext_ref[0:hist, :] = jnp.zeros((hist, CONV_DIM), BF16)
        h_ref[...] = jnp.zeros_like(h_ref)

    @pl.when(c > 0)
    def _():
        ext_ref[0:hist, :] = ext_ref[L:L + hist, :]

    ext_ref[hist:hist + L, :] = ub
    srow = lax.broadcasted_iota(jnp.int32, ((CONV_W - 1) * L, hist + L), 0)
    scol = lax.broadcasted_iota(jnp.int32, ((CONV_W - 1) * L, hist + L), 1)
    blk = sum((srow >= s * L).astype(jnp.int32) for s in range(1, CONV_W - 1))
    sel = scol == srow + (hist - 1) - blk * (L + 1)
    shifted = _dot(sel.astype(BF16), ext_ref[...])
    acc = cb_ref[...] + u * cw_ref[CONV_W - 1:CONV_W, :]
    for s in range(1, CONV_W):
        acc = acc + shifted[(s - 1) * L:s * L, :] * cw_ref[CONV_W - 1 - s:CONV_W - s, :]
    act = _silu(acc)

    tail = sm_ref[0, :, SM_TAIL:SM_TAIL + LANES]
    dt = _softplus(tail + dtb_ref[...])
    da = dt * (-jnp.exp(alog_ref[...]))
    row = lax.broadcasted_iota(jnp.int32, (L, L), 0)
    col = lax.broadcasted_iota(jnp.int32, (L, L), 1)
    causal = row >= col
    tri = causal.astype(F32)
    acum = jnp.dot(tri, da, precision=lax.Precision.HIGHEST, preferred_element_type=F32) * math.log2(math.e)
    acum_t = acum.T
    a_last = acum[L - 1:L, :]
    to_end = jnp.exp2(a_last - acum)
    e_acum = jnp.exp2(acum)
    cdec = jnp.exp2(a_last)
    lane = lax.broadcasted_iota(jnp.int32, (L, LANES), 1)

    def expand_group(v, g):
        return jnp.concatenate([_expand_heads(v, lane, 2 * g), _expand_heads(v, lane, 2 * g + 1)], axis=1)

    for g in range(SSM_GROUPS):
        xs_g = act[:, g * rp:(g + 1) * rp]
        bb = act[:, D_INNER + g * D_STATE:D_INNER + (g + 1) * D_STATE].astype(BF16)
        cbf = act[:, D_INNER + gn + g * D_STATE:D_INNER + gn + (g + 1) * D_STATE].astype(BF16)
        xdt = xs_g * expand_group(dt, g)
        xdtb = xdt.astype(BF16)
        cb = _dot_nt(cbf, bb)
        yd = []
        for r in range(HEADS_PER_GROUP):
            hd = DT_OFF + g * HEADS_PER_GROUP + r
            seg = acum[:, hd:hd + 1] - acum_t[hd:hd + 1, :]
            dec = jnp.exp2(jnp.where(causal, seg, NEG))
            yd.append(_dot((cb * dec).astype(BF16), xdtb[:, r * SSM_HEADDIM:(r + 1) * SSM_HEADDIM]))
        y_diag = jnp.concatenate(yd, axis=1)
        hg = h_ref[g * rp:(g + 1) * rp, :]
        y_off = _dot_nt(cbf, hg.astype(BF16)) * expand_group(e_acum, g)
        xw = (xdt * expand_group(to_end, g)).astype(BF16)
        st = _dot_tn(xw, bb)
        for r in range(HEADS_PER_GROUP):
            hd = DT_OFF + g * HEADS_PER_GROUP + r
            lo = g * rp + r * SSM_HEADDIM
            h_ref[lo:lo + SSM_HEADDIM, :] = (cdec[0:1, hd:hd + 1] * hg[r * SSM_HEADDIM:(r + 1) * SSM_HEADDIM, :]
                                             + st[r * SSM_HEADDIM:(r + 1) * SSM_HEADDIM, :])
        ybuf_ref[:, g * rp:(g + 1) * rp] = y_diag + y_off + xs_g * dexp_ref[:, g * rp:(g + 1) * rp]

    yz = ybuf_ref[...] * _silu(z_ref[0].astype(F32))
    y_ref[0] = _rms(yz, ng_ref[...]).astype(y_ref.dtype)

    @pl.when(c == nc - 1)
    def _():
        hout_ref[0] = h_ref[...]
        cout_ref[0] = u[L - (CONV_W - 1):L, :]


def _ssd_prompt(big, small, lw):
    nb, t, _ = big.shape
    L = SSD_CHUNK if t % SSD_CHUNK == 0 else t
    nc = t // L
    kern = functools.partial(_ssd_kernel, L=L)
    const = lambda b, c: (0, 0)
    return pl.pallas_call(
        kern,
        grid=(nb, nc),
        in_specs=[
            pl.BlockSpec((1, L, CONV_DIM), lambda b, c: (b, c, 0)),
            pl.BlockSpec((1, L, D_INNER), lambda b, c: (b, c, 2)),
            pl.BlockSpec((1, L, SM_W), lambda b, c: (b, c, 0)),
            pl.BlockSpec((CONV_W, CONV_DIM), const),
            pl.BlockSpec((1, CONV_DIM), const),
            pl.BlockSpec((1, LANES), const),
            pl.BlockSpec((1, LANES), const),
            pl.BlockSpec((1, D_INNER), const),
            pl.BlockSpec((1, D_INNER), const),
        ],
        out_specs=[
            pl.BlockSpec((1, L, D_INNER), lambda b, c: (b, c, 0)),
            pl.BlockSpec((1, D_INNER, D_STATE), lambda b, c: (b, 0, 0)),
            pl.BlockSpec((1, CONV_W - 1, CONV_DIM), lambda b, c: (b, 0, 0)),
        ],
        out_shape=[
            jax.ShapeDtypeStruct((nb, t, D_INNER), BF16),
            jax.ShapeDtypeStruct((nb, D_INNER, D_STATE), F32),
            jax.ShapeDtypeStruct((nb, CONV_W - 1, CONV_DIM), F32),
        ],
        scratch_shapes=[
            pltpu.VMEM((L + 2 * SUBLANES, CONV_DIM), BF16),
            pltpu.VMEM((D_INNER, D_STATE), F32),
            pltpu.VMEM((L, D_INNER), F32),
        ],
        compiler_params=_params("arbitrary", "arbitrary"),
        name="ssd_prompt",
    )(big, big, small, lw["conv_w"], lw["conv_b"], lw["dt_bias"], lw["a_log"], lw["d_exp"], lw["ssm_norm_g"])


def _ssd_step_kernel(xbc_ref, z_ref, sm_ref, cs_ref, hs_ref, cw_ref, cb_ref, dtb_ref, alog_ref, dexp_ref, ng_ref,
                     y_ref, hout_ref, cout_ref, *, n_seq):
    gn = SSM_GROUPS * D_STATE
    lane = lax.broadcasted_iota(jnp.int32, (1, LANES), 1)
    neg_a = -jnp.exp(alog_ref[...])

    for q in range(n_seq):
        u = xbc_ref[q:q + 1, :].astype(F32)
        prev = cs_ref[q]
        acc = cb_ref[...] + u * cw_ref[CONV_W - 1:CONV_W, :]
        for k in range(CONV_W - 1):
            acc = acc + prev[k:k + 1, :] * cw_ref[k:k + 1, :]
        cout_ref[q, 0:CONV_W - 2, :] = prev[1:CONV_W - 1, :]
        cout_ref[q, CONV_W - 2:CONV_W - 1, :] = u
        act = _silu(acc)

        tail = sm_ref[q:q + 1, SM_TAIL:SM_TAIL + LANES]
        dt = _softplus(tail + dtb_ref[...])
        d_a = jnp.exp(dt * neg_a)
        ys = []
        for j in range(D_INNER // LANES):
            g = (j * LANES) // (HEADS_PER_GROUP * SSM_HEADDIM)
            xs_j = act[:, j * LANES:(j + 1) * LANES]
            xdt = xs_j * _expand_heads(dt, lane, j)
            xcol = jnp.broadcast_to(xdt, (LANES, LANES)).T
            dcol = jnp.broadcast_to(_expand_heads(d_a, lane, j), (LANES, LANES)).T
            b_g = act[:, D_INNER + g * D_STATE:D_INNER + (g + 1) * D_STATE]
            c_g = act[:, D_INNER + gn + g * D_STATE:D_INNER + gn + (g + 1) * D_STATE]
            hnew = dcol * hs_ref[q, j * LANES:(j + 1) * LANES, :] + xcol * b_g
            hout_ref[q, j * LANES:(j + 1) * LANES, :] = hnew
            yrow = jnp.sum((hnew * c_g).T, axis=0, keepdims=True)
            ys.append(yrow + xs_j * dexp_ref[:, j * LANES:(j + 1) * LANES])
        y = jnp.concatenate(ys, axis=1)
        yz = y * _silu(z_ref[q:q + 1, :].astype(F32))
        y_ref[q:q + 1, :] = _rms(yz, ng_ref[...]).astype(y_ref.dtype)


def _ssd_step(layer, big, small, conv_state, ssm_state, lw):
    n = big.shape[0]
    sq = math.gcd(SUBLANES, n)
    const = lambda s: (0, 0)
    return pl.pallas_call(
        functools.partial(_ssd_step_kernel, n_seq=sq),
        grid=(n // sq,),
        in_specs=[
            pl.BlockSpec((sq, CONV_DIM), lambda s: (s, 0)),
            pl.BlockSpec((sq, D_INNER), lambda s: (s, 2)),
            pl.BlockSpec((sq, SM_W), lambda s: (s, 0)),
            pl.BlockSpec((sq, CONV_W - 1, CONV_DIM), lambda s: (s, 0, 0)),
            pl.BlockSpec((None, sq, D_INNER, D_STATE), lambda s: (layer, s, 0, 0)),
            pl.BlockSpec((CONV_W, CONV_DIM), const),
            pl.BlockSpec((1, CONV_DIM), const),
            pl.BlockSpec((1, LANES), const),
            pl.BlockSpec((1, LANES), const),
            pl.BlockSpec((1, D_INNER), const),
            pl.BlockSpec((1, D_INNER), const),
        ],
        out_specs=[
            pl.BlockSpec((sq, D_INNER), lambda s: (s, 0)),
            pl.BlockSpec((sq, D_INNER, D_STATE), lambda s: (s, 0, 0)),
            pl.BlockSpec((sq, CONV_W - 1, CONV_DIM), lambda s: (s, 0, 0)),
        ],
        out_shape=[
            jax.ShapeDtypeStruct((n, D_INNER), F32),
            jax.ShapeDtypeStruct((n, D_INNER, D_STATE), F32),
            jax.ShapeDtypeStruct((n, CONV_W - 1, CONV_DIM), F32),
        ],
        compiler_params=_params("arbitrary"),
        name="ssd_step",
    )(big, big, small, conv_state, ssm_state, lw["conv_w"], lw["conv_b"], lw["dt_bias"], lw["a_log"],
      lw["d_exp"], lw["ssm_norm_g"])


def _prep_kernel(sm_ref, cos_ref, sin_ref, qg_ref, kvg_ref, wqn_ref, wqr_ref, wkt_ref,
                 qlat_ref, qrope_ref, kc_ref, kr_ref, latf_ref, krf_ref, kct_ref, krt_ref, *, scale):
    sm = sm_ref[0]
    tm = sm.shape[0]
    half = QK_ROPE // 2
    qn = _rms(sm[:, SM_QA:SM_QA + Q_LORA], qg_ref[...]).astype(BF16)
    qnope = _dot(qn, wqn_ref[...])
    qr = _dot(qn, wqr_ref[...])
    cos = cos_ref[...]
    sin = sin_ref[...]
    wq = MLA_HEADS * QK_ROPE
    reps = wq // LANES
    cosq = jnp.concatenate([cos] * reps, axis=1)
    sinq = jnp.concatenate([sin] * reps, axis=1)
    lane_q = lax.broadcasted_iota(jnp.int32, (tm, wq), 1)
    swap = jnp.where((lane_q & (QK_ROPE - 1)) < half, pltpu.roll(qr, wq - half, 1), pltpu.roll(qr, half, 1))
    qrot = (qr * cosq + swap * sinq) * scale
    for h in range(MLA_HEADS):
        qrope_ref[0, h] = qrot[:, h * QK_ROPE:(h + 1) * QK_ROPE].astype(BF16)
        ql = _dot(qnope[:, h * QK_NOPE:(h + 1) * QK_NOPE].astype(BF16), wkt_ref[h]) * scale
        qlat_ref[0, h] = ql.astype(BF16)
    cn = _rms(sm[:, SM_LAT:SM_LAT + KV_LORA], kvg_ref[...])
    latf_ref[0] = cn
    kc_ref[0] = cn.astype(BF16)
    kct_ref[0] = cn.T.astype(BF16)
    tail = sm[:, SM_TAIL:SM_TAIL + LANES]
    lane_k = lax.broadcasted_iota(jnp.int32, (tm, LANES), 1)
    swapk = jnp.where(lane_k < half, pltpu.roll(tail, LANES - half, 1), pltpu.roll(tail, half, 1))
    krot = tail * cos + swapk * sin
    krf_ref[0] = krot[:, 0:QK_ROPE]
    kr_ref[0] = krot[:, 0:QK_ROPE].astype(BF16)
    krt_ref[0] = krot.T[0:QK_ROPE, :].astype(BF16)


def _mla_prep(small, cos, sin, lw, tm=256):
    nb, t, _ = small.shape
    tm = min(tm, t)
    scale = (QK_NOPE + QK_ROPE) ** -0.5 * math.log2(math.e)
    const2 = lambda b, i: (0, 0)
    return pl.pallas_call(
        functools.partial(_prep_kernel, scale=scale),
        grid=(nb, t // tm),
        in_specs=[
            pl.BlockSpec((1, tm, SM_W), lambda b, i: (b, i, 0)),
            pl.BlockSpec((tm, LANES), lambda b, i: (i, 0)),
            pl.BlockSpec((tm, LANES), lambda b, i: (i, 0)),
            pl.BlockSpec((1, Q_LORA), const2),
            pl.BlockSpec((1, KV_LORA), const2),
            pl.BlockSpec((Q_LORA, MLA_HEADS * QK_NOPE), const2),
            pl.BlockSpec((Q_LORA, MLA_HEADS * QK_ROPE), const2),
            pl.BlockSpec((MLA_HEADS, QK_NOPE, KV_LORA), lambda b, i: (0, 0, 0)),
        ],
        out_specs=[
            pl.BlockSpec((1, MLA_HEADS, tm, KV_LORA), lambda b, i: (b, 0, i, 0)),
            pl.BlockSpec((1, MLA_HEADS, tm, QK_ROPE), lambda b, i: (b, 0, i, 0)),
            pl.BlockSpec((1, tm, KV_LORA), lambda b, i: (b, i, 0)),
            pl.BlockSpec((1, tm, QK_ROPE), lambda b, i: (b, i, 0)),
            pl.BlockSpec((1, tm, KV_LORA), lambda b, i: (b, i, 0)),
            pl.BlockSpec((1, tm, QK_ROPE), lambda b, i: (b, i, 0)),
            pl.BlockSpec((1, KV_LORA, tm), lambda b, i: (b, 0, i)),
            pl.BlockSpec((1, QK_ROPE, tm), lambda b, i: (b, 0, i)),
        ],
        out_shape=[
            jax.ShapeDtypeStruct((nb, MLA_HEADS, t, KV_LORA), BF16),
            jax.ShapeDtypeStruct((nb, MLA_HEADS, t, QK_ROPE), BF16),
            jax.ShapeDtypeStruct((nb, t, KV_LORA), BF16),
            jax.ShapeDtypeStruct((nb, t, QK_ROPE), BF16),
            jax.ShapeDtypeStruct((nb, t, KV_LORA), F32),
            jax.ShapeDtypeStruct((nb, t, QK_ROPE), F32),
            jax.ShapeDtypeStruct((nb, KV_LORA, t), BF16),
            jax.ShapeDtypeStruct((nb, QK_ROPE, t), BF16),
        ],
        compiler_params=_params("arbitrary", "arbitrary"),
        name="mla_prep",
    )(small, cos, sin, lw["q_norm_g"], lw["kv_norm_g"], lw["w_q_nope"], lw["w_q_rope"], lw["w_k_t"])


def _lane_tiles(x):
    return [x[:, k * LANES:(k + 1) * LANES] for k in range(x.shape[1] // LANES)]


def _softmax_step(s, m_old=None, l_old=None):
    tiles = _lane_tiles(s)
    rows = s.shape[0]
    row_max = jnp.broadcast_to(jnp.max(functools.reduce(jnp.maximum, tiles), axis=1, keepdims=True), (rows, LANES))
    m_new = row_max if m_old is None else jnp.maximum(m_old, row_max)
    p_tiles = [jnp.exp2(t - m_new) for t in tiles]
    l_new = jnp.broadcast_to(jnp.sum(functools.reduce(jnp.add, p_tiles), axis=1, keepdims=True), (rows, LANES))
    alpha = None
    if m_old is not None:
        alpha = jnp.exp2(m_old - m_new)
        l_new = alpha * l_old + l_new
    return m_new, alpha, l_new, jnp.concatenate(p_tiles, axis=1)


def _attn_kernel(qlat_ref, qrope_ref, kc_ref, kct_ref, krt_ref, wv_ref, o_ref, m_ref, l_ref, acc_ref, *, tq, tk, hc):
    i = pl.program_id(1)
    cr = hc * tq
    reps = KV_LORA // LANES

    def block(j, diagonal):
        start = pl.multiple_of(j * tk, tk)
        kc = kc_ref[0, pl.ds(start, tk), :]
        kct = kct_ref[0, :, pl.ds(start, tk)]
        krt = krt_ref[0, :, pl.ds(start, tk)]
        if diagonal:
            qpos = i * tq + lax.broadcasted_iota(jnp.int32, (hc, tq, tk), 1).reshape(cr, tk)
            kpos = j * tk + lax.broadcasted_iota(jnp.int32, (cr, tk), 1)
            visible = kpos <= qpos
        for c in range(MLA_HEADS // hc):
            r0 = c * cr
            q1 = qlat_ref[0, c * hc:(c + 1) * hc].reshape(cr, KV_LORA)
            q2 = qrope_ref[0, c * hc:(c + 1) * hc].reshape(cr, QK_ROPE)
            s = _dot(q1, kct) + _dot(q2, krt)
            if diagonal:
                s = jnp.where(visible, s, NEG)
                m_old = l_old = None
            else:
                m_old = m_ref[r0:r0 + cr, :]
                l_old = l_ref[r0:r0 + cr, :]
            m_new, alpha, l_new, p = _softmax_step(s, m_old, l_old)
            l_ref[r0:r0 + cr, :] = l_new
            m_ref[r0:r0 + cr, :] = m_new
            pv = _dot(p.astype(BF16), kc)
            if diagonal:
                acc_ref[r0:r0 + cr, :] = pv
            else:
                acc_ref[r0:r0 + cr, :] = jnp.concatenate([alpha] * reps, axis=1) * acc_ref[r0:r0 + cr, :] + pv

    nfull = (i * tq) // tk
    block(nfull, True)

    def body(j, carry):
        block(j, False)
        return carry

    lax.fori_loop(0, nfull, body, 0)
    for h in range(MLA_HEADS):
        l_h = jnp.concatenate([l_ref[h * tq:(h + 1) * tq, :]] * reps, axis=1)
        o_lat = (acc_ref[h * tq:(h + 1) * tq, :] / l_h).astype(BF16)
        o_ref[0, :, h * V_DIM:(h + 1) * V_DIM] = _dot(o_lat, wv_ref[h]).astype(o_ref.dtype)


def _attn_prompt(qlat, qrope, kc, kct, krt, wv, tq=128, tk=256, hc=16):
    nb, _, t, _ = qlat.shape
    tk = min(tk, t)
    tq = min(tq, tk)
    rows = MLA_HEADS * tq
    return pl.pallas_call(
        functools.partial(_attn_kernel, tq=tq, tk=tk, hc=hc),
        grid=(nb, t // tq),
        in_specs=[
            pl.BlockSpec((1, MLA_HEADS, tq, KV_LORA), lambda b, i: (b, 0, i, 0)),
            pl.BlockSpec((1, MLA_HEADS, tq, QK_ROPE), lambda b, i: (b, 0, i, 0)),
            pl.BlockSpec((1, t, KV_LORA), lambda b, i: (b, 0, 0)),
            pl.BlockSpec((1, KV_LORA, t), lambda b, i: (b, 0, 0)),
            pl.BlockSpec((1, QK_ROPE, t), lambda b, i: (b, 0, 0)),
            pl.BlockSpec((MLA_HEADS, KV_LORA, V_DIM), lambda b, i: (0, 0, 0)),
        ],
        out_specs=pl.BlockSpec((1, tq, MLA_HEADS * V_DIM), lambda b, i: (b, i, 0)),
        out_shape=jax.ShapeDtypeStruct((nb, t, MLA_HEADS * V_DIM), BF16),
        scratch_shapes=[pltpu.VMEM((rows, LANES), F32), pltpu.VMEM((rows, LANES), F32),
                        pltpu.VMEM((rows, KV_LORA), F32)],
        compiler_params=_params("arbitrary", "arbitrary"),
        name="attn_prompt",
    )(qlat, qrope, kc, kct, krt, wv)


def _decode_kernel(pt_ref, qlat_ref, qrope_ref, kcn_ref, krn_ref, lat_hbm, krt_hbm, o_ref,
                   lat_buf, krt_buf, s_buf, sem, *, layer, npages, chunk_pages):
    s = pl.program_id(0)
    n = pl.num_programs(0)
    slot = lax.rem(s, 2)
    ck = chunk_pages * PAGE_SIZE
    nchunk = npages // chunk_pages

    def page_copies(seq, slot_, p):
        pg = pt_ref[seq * npages + p]
        keys = pl.ds(p * PAGE_SIZE, PAGE_SIZE)
        return (pltpu.make_async_copy(lat_hbm.at[layer, pg], lat_buf.at[slot_, keys, :], sem.at[0, slot_]),
                pltpu.make_async_copy(krt_hbm.at[layer, pg], krt_buf.at[slot_, :, keys], sem.at[1, slot_]))

    def start_all(seq, slot_):
        for p in range(npages):
            for cp in page_copies(seq, slot_, p):
                cp.start()

    @pl.when(s == 0)
    def _():
        start_all(0, 0)

    @pl.when(s + 1 < n)
    def _():
        start_all(s + 1, 1 - slot)

    for p in range(npages):
        for cp in page_copies(s, slot, p):
            cp.wait()

    q1 = qlat_ref[0]
    q2 = qrope_ref[0]
    kn = kcn_ref[0].astype(F32)
    s_new = (jnp.sum(q1.astype(F32) * kn, axis=1, keepdims=True)
             + jnp.sum(q2.astype(F32) * krn_ref[0].astype(F32), axis=1, keepdims=True))

    def chunk_keys(c):
        lat = lat_buf[slot, c * ck:(c + 1) * ck, :].astype(BF16)
        krt = krt_buf[slot, :, c * ck:(c + 1) * ck].astype(BF16)
        return lat, krt

    mx = jnp.broadcast_to(s_new, (MLA_HEADS, LANES))
    for c in range(nchunk):
        lat, krt = chunk_keys(c)
        sc = _dot_nt(q1, lat) + _dot(q2, krt)
        s_buf[:, c * ck:(c + 1) * ck] = sc
        mx = jnp.maximum(mx, functools.reduce(jnp.maximum, _lane_tiles(sc)))
    m = jnp.max(mx, axis=1, keepdims=True)
    p_new = jnp.exp2(s_new - m)
    l = p_new
    acc = p_new * kn
    for c in range(nchunk):
        lat, _ = chunk_keys(c)
        p = jnp.exp2(s_buf[:, c * ck:(c + 1) * ck] - m)
        l = l + jnp.sum(p, axis=1, keepdims=True)
        acc = acc + _dot(p.astype(BF16), lat)
    o_ref[0] = (acc / l).astype(o_ref.dtype)


def _attn_decode(layer, page_table, cache_latent, cache_krope_t, qlat, qrope, kcn, krn, chunk_pages=8):
    n, npages = page_table.shape
    chunk_pages = math.gcd(chunk_pages, npages)
    n_keys = npages * PAGE_SIZE
    pt = page_table.reshape(-1)
    row = lambda s, pt_ref: (s, 0, 0)
    grid_spec = pltpu.PrefetchScalarGridSpec(
        num_scalar_prefetch=1,
        grid=(n,),
        in_specs=[
            pl.BlockSpec((1, MLA_HEADS, KV_LORA), row),
            pl.BlockSpec((1, MLA_HEADS, QK_ROPE), row),
            pl.BlockSpec((1, 1, KV_LORA), row),
            pl.BlockSpec((1, 1, QK_ROPE), row),
            pl.BlockSpec(memory_space=pl.ANY),
            pl.BlockSpec(memory_space=pl.ANY),
        ],
        out_specs=pl.BlockSpec((1, MLA_HEADS, KV_LORA), row),
        scratch_shapes=[
            pltpu.VMEM((2, n_keys, KV_LORA), F32),
            pltpu.VMEM((2, QK_ROPE, n_keys), F32),
            pltpu.VMEM((MLA_HEADS, n_keys), F32),
            pltpu.SemaphoreType.DMA((2, 2)),
        ],
    )
    return pl.pallas_call(
        functools.partial(_decode_kernel, layer=layer, npages=npages, chunk_pages=chunk_pages),
        grid_spec=grid_spec,
        out_shape=jax.ShapeDtypeStruct((n, MLA_HEADS, KV_LORA), BF16),
        compiler_params=_params("arbitrary"),
        name="attn_decode",
    )(pt, qlat, qrope, kcn, krn, cache_latent, cache_krope_t)


def _vproj_kernel(o_ref, wv_ref, y_ref):
    for h in range(MLA_HEADS):
        y_ref[:, h * V_DIM:(h + 1) * V_DIM] = _dot(o_ref[h], wv_ref[h]).astype(y_ref.dtype)


def _vproj(o_lat, wv):
    n = o_lat.shape[1]
    return pl.pallas_call(
        _vproj_kernel,
        out_shape=jax.ShapeDtypeStruct((n, MLA_HEADS * V_DIM), BF16),
        name="v_proj",
    )(o_lat, wv)


def _merge_kernel(ys_ref, ym_ref, g_ref, x_ref, wbs_ref, wbm_ref, wo_ref, lg_ref, lb_ref, o_ref):
    gates = g_ref[...].astype(F32)
    a = _dot(ys_ref[...].astype(BF16), wbs_ref[...])
    b = _dot(ym_ref[...], wbm_ref[...])
    m = jax.nn.sigmoid(gates[:, 0:D_MODEL]) * a + jax.nn.sigmoid(gates[:, D_MODEL:2 * D_MODEL]) * b
    r = ALPHA * x_ref[...] + _dot(m.astype(BF16), wo_ref[...])
    o_ref[...] = _layernorm(r, lg_ref[...], lb_ref[...])


def _merge(y_ssm, y_mla, big, x, lw, tm=512):
    m = x.shape[0]
    tm = min(tm, m)
    const = lambda i: (0, 0)
    return pl.pallas_call(
        _merge_kernel,
        grid=(m // tm,),
        in_specs=[
            pl.BlockSpec((tm, D_INNER), lambda i: (i, 0)),
            pl.BlockSpec((tm, MLA_HEADS * V_DIM), lambda i: (i, 0)),
            pl.BlockSpec((tm, 2 * D_MODEL), lambda i: (i, 3)),
            pl.BlockSpec((tm, D_MODEL), lambda i: (i, 0)),
            pl.BlockSpec((D_INNER, D_MODEL), const),
            pl.BlockSpec((MLA_HEADS * V_DIM, D_MODEL), const),
            pl.BlockSpec((D_MODEL, D_MODEL), const),
            pl.BlockSpec((1, D_MODEL), const),
            pl.BlockSpec((1, D_MODEL), const),
        ],
        out_specs=pl.BlockSpec((tm, D_MODEL), lambda i: (i, 0)),
        out_shape=jax.ShapeDtypeStruct((m, D_MODEL), F32),
        compiler_params=_params("arbitrary"),
        name="merge_ln",
    )(y_ssm, y_mla, big, x, lw["w_br_ssm"], lw["w_br_mla"], lw["w_out"], lw["ln1_g"], lw["ln1_b"])


def _swiglu_hidden(xb, wgu):
    tf = wgu.shape[1] // 2
    gu = _dot(xb, wgu)
    return (_silu(gu[:, 0:tf]) * gu[:, tf:2 * tf]).astype(BF16)


def _fuse_gate_up(wg, wu, tf):
    pieces = []
    for j in range(wg.shape[-1] // tf):
        pieces += [wg[..., j * tf:(j + 1) * tf], wu[..., j * tf:(j + 1) * tf]]
    return jnp.concatenate(pieces, axis=-1)


def _ffn_kernel(x_ref, wgu_ref, wd_ref, lg_ref, lb_ref, o_ref, acc_ref):
    j = pl.program_id(1)
    h = _swiglu_hidden(x_ref[...].astype(BF16), wgu_ref[...])
    part = _dot(h, wd_ref[...])

    @pl.when(j == 0)
    def _():
        acc_ref[...] = part

    @pl.when(j > 0)
    def _():
        acc_ref[...] += part

    @pl.when(j == pl.num_programs(1) - 1)
    def _():
        o_ref[...] = _layernorm(ALPHA * x_ref[...] + acc_ref[...], lg_ref[...], lb_ref[...])


def _ff_tile(d_ff):
    half = d_ff // 2
    return half if half % LANES == 0 else d_ff


def _ffn(x, wgu, wd, ln_g, ln_b, tm=512):
    m = x.shape[0]
    d_ff = wd.shape[0]
    tm = min(tm, m)
    tf = _ff_tile(d_ff)
    const = lambda i, j: (0, 0)
    return pl.pallas_call(
        _ffn_kernel,
        grid=(m // tm, d_ff // tf),
        in_specs=[
            pl.BlockSpec((tm, D_MODEL), lambda i, j: (i, 0)),
            pl.BlockSpec((D_MODEL, 2 * tf), lambda i, j: (0, j)),
            pl.BlockSpec((tf, D_MODEL), lambda i, j: (j, 0)),
            pl.BlockSpec((1, D_MODEL), const),
            pl.BlockSpec((1, D_MODEL), const),
        ],
        out_specs=pl.BlockSpec((tm, D_MODEL), lambda i, j: (i, 0)),
        out_shape=jax.ShapeDtypeStruct((m, D_MODEL), F32),
        scratch_shapes=[pltpu.VMEM((tm, D_MODEL), F32)],
        compiler_params=_params("arbitrary", "arbitrary"),
        name="ffn_ln",
    )(x, wgu, wd, ln_g, ln_b)


def _router_kernel(x_ref, rw_ref, c_ref):
    logits = jnp.dot(x_ref[...], rw_ref[...], precision=lax.Precision.HIGHEST, preferred_element_type=F32)
    lane = lax.broadcasted_iota(jnp.int32, logits.shape, 1)
    logits = jnp.where(lane < N_EXPERTS, logits, NEG)
    m1 = jnp.max(logits, axis=1, keepdims=True)
    i1 = jnp.min(jnp.where(logits == m1, lane, LANES), axis=1, keepdims=True)
    rest = jnp.where(lane == i1, NEG, logits)
    m2 = jnp.max(rest, axis=1, keepdims=True)
    i2 = jnp.min(jnp.where(rest == m2, lane, LANES), axis=1, keepdims=True)
    e2 = jnp.exp(m2 - m1)
    den = 1.0 + e2
    c_ref[...] = jnp.where(lane == i1, 1.0 / den, 0.0) + jnp.where(lane == i2, e2 / den, 0.0)


def _router(x, rw_pad, tm=512):
    m = x.shape[0]
    tm = min(tm, m)
    return pl.pallas_call(
        _router_kernel,
        grid=(m // tm,),
        in_specs=[pl.BlockSpec((tm, D_MODEL), lambda i: (i, 0)),
                  pl.BlockSpec((D_MODEL, LANES), lambda i: (0, 0))],
        out_specs=pl.BlockSpec((tm, LANES), lambda i: (i, 0)),
        out_shape=jax.ShapeDtypeStruct((m, LANES), F32),
        compiler_params=_params("arbitrary"),
        name="router",
    )(x, rw_pad)


def _moe_kernel(x_ref, c_ref, wgu_ref, wd_ref, lg_ref, lb_ref, o_ref, acc_ref):
    e = pl.program_id(1)
    j = pl.program_id(2)
    h = _swiglu_hidden(x_ref[...].astype(BF16), wgu_ref[0])
    comb = c_ref[...]
    lane = lax.broadcasted_iota(jnp.int32, comb.shape, 1)
    w_e = jnp.sum(jnp.where(lane == e, comb, 0.0), axis=1, keepdims=True)
    part = w_e * _dot(h, wd_ref[0])
    first = jnp.logical_and(e == 0, j == 0)

    @pl.when(first)
    def _():
        acc_ref[...] = part

    @pl.when(jnp.logical_not(first))
    def _():
        acc_ref[...] += part

    @pl.when(jnp.logical_and(e == pl.num_programs(1) - 1, j == pl.num_programs(2) - 1))
    def _():
        o_ref[...] = _layernorm(ALPHA * x_ref[...] + acc_ref[...], lg_ref[...], lb_ref[...])


def _moe(x, comb, wgu, wd, ln_g, ln_b, tm=512):
    m = x.shape[0]
    n_e, d_ff, _ = wd.shape
    tm = min(tm, m)
    tf = _ff_tile(d_ff)
    const = lambda i, e, j: (0, 0)
    return pl.pallas_call(
        _moe_kernel,
        grid=(m // tm, n_e, d_ff // tf),
        in_specs=[
            pl.BlockSpec((tm, D_MODEL), lambda i, e, j: (i, 0)),
            pl.BlockSpec((tm, LANES), lambda i, e, j: (i, 0)),
            pl.BlockSpec((1, D_MODEL, 2 * tf), lambda i, e, j: (e, 0, j)),
            pl.BlockSpec((1, tf, D_MODEL), lambda i, e, j: (e, j, 0)),
            pl.BlockSpec((1, D_MODEL), const),
            pl.BlockSpec((1, D_MODEL), const),
        ],
        out_specs=pl.BlockSpec((tm, D_MODEL), lambda i, e, j: (i, 0)),
        out_shape=jax.ShapeDtypeStruct((m, D_MODEL), F32),
        scratch_shapes=[pltpu.VMEM((tm, D_MODEL), F32)],
        compiler_params=_params("arbitrary", "arbitrary", "arbitrary"),
        name="moe_ln",
    )(x, comb, wgu, wd, ln_g, ln_b)


def _layer_weights(l, w_in, conv_w, conv_b, dt_bias, a_log, d_skip, ssm_norm_g, q_norm_g, w_qb, kv_norm_g,
                   w_kvb, w_br_ssm, w_br_mla, w_out, ln1_g, ln1_b, ln2_g, ln2_b):
    wi = w_in[l]
    o_xbc = D_INNER
    o_dt = o_xbc + CONV_DIM
    o_qa = o_dt + SSM_HEADS
    o_kv = o_qa + Q_LORA
    o_gs = o_kv + KV_LORA + QK_ROPE
    o_gm = o_gs + D_MODEL
    w_big = jnp.concatenate([wi[:, o_xbc:o_dt], wi[:, 0:o_xbc], wi[:, o_gs:o_gm], wi[:, o_gm:]], axis=1).astype(BF16)
    w_small = jnp.concatenate(
        [wi[:, o_qa:o_kv], wi[:, o_kv:o_gs], wi[:, o_dt:o_qa],
         jnp.zeros((D_MODEL, LANES - QK_ROPE - SSM_HEADS), F32)], axis=1).astype(BF16)

    def tail_pad(v):
        return jnp.zeros((1, LANES), F32).at[0, DT_OFF:DT_OFF + SSM_HEADS].set(v)

    wq = w_qb[l].reshape(Q_LORA, MLA_HEADS, QK_NOPE + QK_ROPE)
    wkv = w_kvb[l].reshape(KV_LORA, MLA_HEADS, QK_NOPE + V_DIM)
    return {
        "w_big": w_big,
        "w_small": w_small,
        "conv_w": conv_w[l],
        "conv_b": conv_b[l].reshape(1, CONV_DIM),
        "dt_bias": tail_pad(dt_bias[l]),
        "a_log": tail_pad(a_log[l]),
        "d_exp": jnp.repeat(d_skip[l], SSM_HEADDIM).reshape(1, D_INNER),
        "ssm_norm_g": ssm_norm_g[l].reshape(1, D_INNER),
        "q_norm_g": q_norm_g[l].reshape(1, Q_LORA),
        "kv_norm_g": kv_norm_g[l].reshape(1, KV_LORA),
        "w_q_nope": wq[:, :, :QK_NOPE].reshape(Q_LORA, MLA_HEADS * QK_NOPE).astype(BF16),
        "w_q_rope": wq[:, :, QK_NOPE:].reshape(Q_LORA, MLA_HEADS * QK_ROPE).astype(BF16),
        "w_k_t": jnp.transpose(wkv[:, :, :QK_NOPE], (1, 2, 0)).astype(BF16),
        "w_v": jnp.transpose(wkv[:, :, QK_NOPE:], (1, 0, 2)).astype(BF16),
        "w_br_ssm": w_br_ssm[l].astype(BF16),
        "w_br_mla": w_br_mla[l].astype(BF16),
        "w_out": w_out[l].astype(BF16),
        "ln1_g": ln1_g[l].reshape(1, D_MODEL),
        "ln1_b": ln1_b[l].reshape(1, D_MODEL),
        "ln2_g": ln2_g[l].reshape(1, D_MODEL),
        "ln2_b": ln2_b[l].reshape(1, D_MODEL),
    }


def _rope_tables(pos):
    half = QK_ROPE // 2
    inv = ROPE_BASE ** (-jnp.arange(0, QK_ROPE, 2, dtype=F32) / QK_ROPE)
    ang = pos.astype(F32)[:, None] * inv[None, :]
    cos, sin = jnp.cos(ang), jnp.sin(ang)
    cos_t = jnp.tile(cos, (1, LANES // half))
    sin_t = jnp.tile(jnp.concatenate([-sin, sin], axis=1), (1, LANES // QK_ROPE))
    return cos_t, sin_t


def kernel(x_prompt, x_sample, cache_latent, cache_krope, state_ssm, state_conv, page_table, w_in, conv_w, conv_b, dt_bias, a_log, d_skip, ssm_norm_g, q_norm_g, w_qb, kv_norm_g, w_kvb, w_br_ssm, w_br_mla, w_out, ln1_g, ln1_b, ln2_g, ln2_b, ffn_w_gate, ffn_w_up, ffn_w_down, router_w, moe_w_gate, moe_w_up, moe_w_down):
    nb, t, _ = x_prompt.shape
    ns = x_sample.shape[0]
    assert x_sample.shape[1] == 1
    n_past = page_table.shape[1] * PAGE_SIZE
    cos_p, sin_p = _rope_tables(jnp.arange(t, dtype=jnp.int32))
    cos_s, sin_s = _rope_tables(jnp.full((ns,), n_past, dtype=jnp.int32))

    cache_krope_t = jnp.swapaxes(cache_krope, 2, 3)
    state_ssm_r = state_ssm.reshape(DEPTH, ns, D_INNER, D_STATE)

    xp = x_prompt.reshape(nb * t, D_MODEL)
    xs = x_sample.reshape(ns, D_MODEL)
    outs = [[] for _ in range(8)]
    for l in range(DEPTH):
        lw = _layer_weights(l, w_in, conv_w, conv_b, dt_bias, a_log, d_skip, ssm_norm_g, q_norm_g, w_qb,
                            kv_norm_g, w_kvb, w_br_ssm, w_br_mla, w_out, ln1_g, ln1_b, ln2_g, ln2_b)
        big_p = _matmul(xp, lw["w_big"], BF16)
        small_p = _matmul(xp, lw["w_small"], F32)
        big_p3 = big_p.reshape(nb, t, -1)
        small_p3 = small_p.reshape(nb, t, SM_W)
        y_ssm_p, h_p, cv_p = _ssd_prompt(big_p3, small_p3, lw)
        qlat, qrope, kc, _, lat_f, kr_f, kct, krt = _mla_prep(small_p3, cos_p, sin_p, lw)
        y_mla_p = _attn_prompt(qlat, qrope, kc, kct, krt, lw["w_v"])
        xp = _merge(y_ssm_p.reshape(nb * t, D_INNER), y_mla_p.reshape(nb * t, -1), big_p, xp, lw)
        big_s = _matmul(xs, lw["w_big"], F32)
        small_s = _matmul(xs, lw["w_small"], F32)
        y_ssm_s, h_s, cv_s = _ssd_step(l, big_s, small_s, state_conv[l], state_ssm_r, lw)
        qlat_s, qrope_s, kc_s, kr_s, lat_fs, kr_fs, _, _ = _mla_prep(small_s.reshape(1, ns, SM_W), cos_s, sin_s, lw)
        o_lat = _attn_decode(l, page_table, cache_latent, cache_krope_t,
                             jnp.transpose(qlat_s[0], (1, 0, 2)), jnp.transpose(qrope_s[0], (1, 0, 2)),
                             kc_s.reshape(ns, 1, KV_LORA), kr_s.reshape(ns, 1, QK_ROPE))
        y_mla_s = _vproj(jnp.transpose(o_lat, (1, 0, 2)), lw["w_v"])
        xs = _merge(y_ssm_s.reshape(ns, D_INNER), y_mla_s, big_s, xs, lw)
        i = l // 2
        if l % 2 == 0:
            tf = _ff_tile(ffn_w_gate.shape[-1])
            wgu = _fuse_gate_up(ffn_w_gate[i].astype(BF16), ffn_w_up[i].astype(BF16), tf)
            wd = ffn_w_down[i].astype(BF16)
            xp = _ffn(xp, wgu, wd, lw["ln2_g"], lw["ln2_b"])
            xs = _ffn(xs, wgu, wd, lw["ln2_g"], lw["ln2_b"])
        else:
            tf = _ff_tile(moe_w_gate.shape[-1])
            wgu = _fuse_gate_up(moe_w_gate[i].astype(BF16), moe_w_up[i].astype(BF16), tf)
            wd = moe_w_down[i].astype(BF16)
            rw = jnp.zeros((D_MODEL, LANES), F32).at[:, :N_EXPERTS].set(router_w[i])
            xp = _moe(xp, _router(xp, rw), wgu, wd, lw["ln2_g"], lw["ln2_b"])
            xs = _moe(xs, _router(xs, rw), wgu, wd, lw["ln2_g"], lw["ln2_b"])
        for k, v in enumerate((lat_f, kr_f, h_p.reshape(nb, SSM_HEADS, SSM_HEADDIM, D_STATE), cv_p,
                               lat_fs.reshape(ns, 1, KV_LORA), kr_fs.reshape(ns, 1, QK_ROPE),
                               h_s.reshape(ns, SSM_HEADS, SSM_HEADDIM, D_STATE), cv_s)):
            outs[k].append(v)

    return (xp.reshape(nb, t, D_MODEL), xs.reshape(ns, 1, D_MODEL)) + tuple(jnp.stack(o) for o in outs)
```

```python
import functools
import math

import jax
import jax.numpy as jnp
from jax import lax
from jax.experimental import pallas as pl
from jax.experimental.pallas import tpu as pltpu

F32 = jnp.float32
BF16 = jnp.bfloat16

D_MODEL = 1024
DEPTH = 4
PAGE_SIZE = 128
D_INNER = 2048
SSM_HEADDIM = 64
SSM_HEADS = 32
SSM_GROUPS = 8
HEADS_PER_GROUP = 4
D_STATE = 128
CONV_W = 4
CONV_DIM = 4096
SSD_CHUNK = 128
MLA_HEADS = 16
QK_NOPE = 128
QK_ROPE = 64
V_DIM = 128
Q_LORA = 256
KV_LORA = 256
ROPE_BASE = 10000.0
N_EXPERTS = 8
ALPHA = (2.0 * DEPTH) ** 0.25
LN_EPS = 1e-5
RMS_EPS = 1e-6

LANES = 128
SUBLANES = 8
VMEM_LIMIT = 56 * 1024 * 1024

SM_QA = 0
SM_LAT = Q_LORA
SM_TAIL = Q_LORA + KV_LORA
SM_W = SM_TAIL + LANES
DT_OFF = QK_ROPE
NEG = -1e30


def _dot(a, b):
    return jnp.dot(a, b, preferred_element_type=F32)


def _dot_nt(a, b):
    return lax.dot_general(a, b, (((1,), (1,)), ((), ())), preferred_element_type=F32)


def _dot_tn(a, b):
    return lax.dot_general(a, b, (((0,), (0,)), ((), ())), preferred_element_type=F32)


def _silu(x):
    h = 0.5 * x
    return h + h * jnp.tanh(h)


def _softplus(x):
    return jnp.maximum(x, 0.0) + jnp.log1p(jnp.exp(-jnp.abs(x)))


def _params(*sem):
    return pltpu.CompilerParams(dimension_semantics=sem, vmem_limit_bytes=VMEM_LIMIT)


def _layernorm(r, g, b):
    mu = jnp.mean(r, axis=-1, keepdims=True)
    d = r - mu
    var = jnp.mean(d * d, axis=-1, keepdims=True)
    return d * lax.rsqrt(var + LN_EPS) * g + b


def _rms(x, g):
    return x * lax.rsqrt(jnp.mean(x * x, axis=-1, keepdims=True) + RMS_EPS) * g


def _mm_kernel(x_ref, w_ref, o_ref):
    o_ref[...] = _dot(x_ref[...].astype(BF16), w_ref[...]).astype(o_ref.dtype)


def _matmul(x, w, out_dtype, tm=1024, tn=1024):
    m, k = x.shape
    n = w.shape[1]
    tm = min(tm, m)
    tn = min(tn, n)
    return pl.pallas_call(
        _mm_kernel,
        grid=(m // tm, n // tn),
        in_specs=[pl.BlockSpec((tm, k), lambda i, j: (i, 0)),
                  pl.BlockSpec((k, tn), lambda i, j: (0, j))],
        out_specs=pl.BlockSpec((tm, tn), lambda i, j: (i, j)),
        out_shape=jax.ShapeDtypeStruct((m, n), out_dtype),
        compiler_params=_params("arbitrary", "arbitrary"),
        name="in_proj",
    )(x, w)


def _expand_heads(col_blk, lane, j):
    a = col_blk[:, DT_OFF + 2 * j:DT_OFF + 2 * j + 1]
    b = col_blk[:, DT_OFF + 2 * j + 1:DT_OFF + 2 * j + 2]
    return jnp.where(lane < SSM_HEADDIM, a, b)


def _ssd_kernel(xbc_ref, z_ref, sm_ref, cw_ref, cb_ref, dtb_ref, alog_ref, dexp_ref, ng_ref, eexp_ref,
                y_ref, hout_ref, cout_ref, ext_ref, h_ref, ybuf_ref, *, L):
    c = pl.program_id(1)
    nc = pl.num_programs(1)
    rp = HEADS_PER_GROUP * SSM_HEADDIM
    gn = SSM_GROUPS * D_STATE
    hist = 2 * SUBLANES

    ub = xbc_ref[0]
    u = ub.astype(F32)

    @pl.when(c == 0)
    def _():
        ext_ref[0:hist, :] = jnp.zeros((hist, CONV_DIM), BF16)
        h_ref[...] = jnp.zeros_like(h_ref)

    @pl.when(c > 0)
    def _():
        ext_ref[0:hist, :] = ext_ref[L:L + hist, :]

    ext_ref[hist:hist + L, :] = ub
    srow = lax.broadcasted_iota(jnp.int32, ((CONV_W - 1) * L, hist + L), 0)
    scol = lax.broadcasted_iota(jnp.int32, ((CONV_W - 1) * L, hist + L), 1)
    blk = sum((srow >= s * L).astype(jnp.int32) for s in range(1, CONV_W - 1))
    sel = scol == srow + (hist - 1) - blk * (L + 1)
    shifted = _dot(sel.astype(BF16), ext_ref[...])
    acc = cb_ref[...] + u * cw_ref[CONV_W - 1:CONV_W, :]
    for s in range(1, CONV_W):
        acc = acc + shifted[(s - 1) * L:s * L, :] * cw_ref[CONV_W - 1 - s:CONV_W - s, :]
    act = _silu(acc)

    tail = sm_ref[0, :, SM_TAIL:SM_TAIL + LANES]
    dt = _softplus(tail + dtb_ref[...])
    da = dt * (-jnp.exp(alog_ref[...]))
    row = lax.broadcasted_iota(jnp.int32, (L, L), 0)
    col = lax.broadcasted_iota(jnp.int32, (L, L), 1)
    causal = row >= col
    tri = causal.astype(F32)
    acum = jnp.dot(tri, da, precision=lax.Precision.HIGHEST, preferred_element_type=F32) * math.log2(math.e)
    acum_t = acum.T
    a_last = acum[L - 1:L, :]
    to_end = jnp.exp2(a_last - acum)
    e_acum = jnp.exp2(acum)
    cdec = jnp.exp2(a_last)

    def hi_lo(v):
        hi = v.astype(BF16)
        return jnp.concatenate([hi, (v - hi.astype(F32)).astype(BF16)], axis=1)

    expanded = _dot(jnp.concatenate([hi_lo(dt), hi_lo(e_acum), hi_lo(to_end)], axis=0), eexp_ref[...])

    def expand_group(k, g):
        return expanded[k * L:(k + 1) * L, g * rp:(g + 1) * rp]

    dt_x, ea_x, te_x = 0, 1, 2
    for g in range(SSM_GROUPS):
        xs_g = act[:, g * rp:(g + 1) * rp]
        bb = act[:, D_INNER + g * D_STATE:D_INNER + (g + 1) * D_STATE].astype(BF16)
        cbf = act[:, D_INNER + gn + g * D_STATE:D_INNER + gn + (g + 1) * D_STATE].astype(BF16)
        xdt = xs_g * expand_group(dt_x, g)
        xdtb = xdt.astype(BF16)
        cb = _dot_nt(cbf, bb)
        yd = []
        for r in range(HEADS_PER_GROUP):
            hd = DT_OFF + g * HEADS_PER_GROUP + r
            seg = acum[:, hd:hd + 1] - acum_t[hd:hd + 1, :]
            dec = jnp.exp2(jnp.where(causal, seg, NEG))
            yd.append(_dot((cb * dec).astype(BF16), xdtb[:, r * SSM_HEADDIM:(r + 1) * SSM_HEADDIM]))
        y_diag = jnp.concatenate(yd, axis=1)
        hg = h_ref[g * rp:(g + 1) * rp, :]
        y_off = _dot_nt(cbf, hg.astype(BF16)) * expand_group(ea_x, g)
        xw = (xdt * expand_group(te_x, g)).astype(BF16)
        st = _dot_tn(xw, bb)
        for r in range(HEADS_PER_GROUP):
            hd = DT_OFF + g * HEADS_PER_GROUP + r
            lo = g * rp + r * SSM_HEADDIM
            h_ref[lo:lo + SSM_HEADDIM, :] = (cdec[0:1, hd:hd + 1] * hg[r * SSM_HEADDIM:(r + 1) * SSM_HEADDIM, :]
                                             + st[r * SSM_HEADDIM:(r + 1) * SSM_HEADDIM, :])
        ybuf_ref[:, g * rp:(g + 1) * rp] = y_diag + y_off + xs_g * dexp_ref[:, g * rp:(g + 1) * rp]

    yz = ybuf_ref[...] * _silu(z_ref[0].astype(F32))
    y_ref[0] = _rms(yz, ng_ref[...]).astype(y_ref.dtype)

    @pl.when(c == nc - 1)
    def _():
        hout_ref[0] = h_ref[...]
        cout_ref[0] = u[L - (CONV_W - 1):L, :]


def _ssd_prompt(big, small, lw):
    nb, t, _ = big.shape
    L = SSD_CHUNK if t % SSD_CHUNK == 0 else t
    nc = t // L
    kern = functools.partial(_ssd_kernel, L=L)
    const = lambda b, c: (0, 0)
    head_of_row = jnp.arange(LANES, dtype=jnp.int32)[:, None] - DT_OFF
    head_of_col = jnp.arange(D_INNER, dtype=jnp.int32)[None, :] // SSM_HEADDIM
    eexp = jnp.tile((head_of_row == head_of_col).astype(BF16), (2, 1))
    return pl.pallas_call(
        kern,
        grid=(nb, nc),
        in_specs=[
            pl.BlockSpec((1, L, CONV_DIM), lambda b, c: (b, c, 0)),
            pl.BlockSpec((1, L, D_INNER), lambda b, c: (b, c, 2)),
            pl.BlockSpec((1, L, SM_W), lambda b, c: (b, c, 0)),
            pl.BlockSpec((CONV_W, CONV_DIM), const),
            pl.BlockSpec((1, CONV_DIM), const),
            pl.BlockSpec((1, LANES), const),
            pl.BlockSpec((1, LANES), const),
            pl.BlockSpec((1, D_INNER), const),
            pl.BlockSpec((1, D_INNER), const),
            pl.BlockSpec((2 * LANES, D_INNER), const),
        ],
        out_specs=[
            pl.BlockSpec((1, L, D_INNER), lambda b, c: (b, c, 0)),
            pl.BlockSpec((1, D_INNER, D_STATE), lambda b, c: (b, 0, 0)),
            pl.BlockSpec((1, CONV_W - 1, CONV_DIM), lambda b, c: (b, 0, 0)),
        ],
        out_shape=[
            jax.ShapeDtypeStruct((nb, t, D_INNER), BF16),
            jax.ShapeDtypeStruct((nb, D_INNER, D_STATE), F32),
            jax.ShapeDtypeStruct((nb, CONV_W - 1, CONV_DIM), F32),
        ],
        scratch_shapes=[
            pltpu.VMEM((L + 2 * SUBLANES, CONV_DIM), BF16),
            pltpu.VMEM((D_INNER, D_STATE), F32),
            pltpu.VMEM((L, D_INNER), F32),
        ],
        compiler_params=_params("arbitrary", "arbitrary"),
        name="ssd_prompt",
    )(big, big, small, lw["conv_w"], lw["conv_b"], lw["dt_bias"], lw["a_log"], lw["d_exp"], lw["ssm_norm_g"], eexp)


def _ssd_step_kernel(xbc_ref, z_ref, sm_ref, cs_ref, hs_ref, cw_ref, cb_ref, dtb_ref, alog_ref, dexp_ref, ng_ref,
                     y_ref, hout_ref, cout_ref, *, n_seq):
    gn = SSM_GROUPS * D_STATE
    lane = lax.broadcasted_iota(jnp.int32, (1, LANES), 1)
    neg_a = -jnp.exp(alog_ref[...])

    for q in range(n_seq):
        u = xbc_ref[q:q + 1, :].astype(F32)
        prev = cs_ref[q]
        acc = cb_ref[...] + u * cw_ref[CONV_W - 1:CONV_W, :]
        for k in range(CONV_W - 1):
            acc = acc + prev[k:k + 1, :] * cw_ref[k:k + 1, :]
        cout_ref[q, 0:CONV_W - 2, :] = prev[1:CONV_W - 1, :]
        cout_ref[q, CONV_W - 2:CONV_W - 1, :] = u
        act = _silu(acc)

        tail = sm_ref[q:q + 1, SM_TAIL:SM_TAIL + LANES]
        dt = _softplus(tail + dtb_ref[...])
        d_a = jnp.exp(dt * neg_a)
        ys = []
        for j in range(D_INNER // LANES):
            g = (j * LANES) // (HEADS_PER_GROUP * SSM_HEADDIM)
            xs_j = act[:, j * LANES:(j + 1) * LANES]
            xdt = xs_j * _expand_heads(dt, lane, j)
            xcol = jnp.broadcast_to(xdt, (LANES, LANES)).T
            dcol = jnp.broadcast_to(_expand_heads(d_a, lane, j), (LANES, LANES)).T
            b_g = act[:, D_INNER + g * D_STATE:D_INNER + (g + 1) * D_STATE]
            c_g = act[:, D_INNER + gn + g * D_STATE:D_INNER + gn + (g + 1) * D_STATE]
            hnew = dcol * hs_ref[q, j * LANES:(j + 1) * LANES, :] + xcol * b_g
            hout_ref[q, j * LANES:(j + 1) * LANES, :] = hnew
            yrow = jnp.sum((hnew * c_g).T, axis=0, keepdims=True)
            ys.append(yrow + xs_j * dexp_ref[:, j * LANES:(j + 1) * LANES])
        y = jnp.concatenate(ys, axis=1)
        yz = y * _silu(z_ref[q:q + 1, :].astype(F32))
        y_ref[q:q + 1, :] = _rms(yz, ng_ref[...]).astype(y_ref.dtype)


def _ssd_step(layer, big, small, conv_state, ssm_state, lw):
    n = big.shape[0]
    sq = math.gcd(SUBLANES, n)
    const = lambda s: (0, 0)
    return pl.pallas_call(
        functools.partial(_ssd_step_kernel, n_seq=sq),
        grid=(n // sq,),
        in_specs=[
            pl.BlockSpec((sq, CONV_DIM), lambda s: (s, 0)),
            pl.BlockSpec((sq, D_INNER), lambda s: (s, 2)),
            pl.BlockSpec((sq, SM_W), lambda s: (s, 0)),
            pl.BlockSpec((sq, CONV_W - 1, CONV_DIM), lambda s: (s, 0, 0)),
            pl.BlockSpec((None, sq, D_INNER, D_STATE), lambda s: (layer, s, 0, 0)),
            pl.BlockSpec((CONV_W, CONV_DIM), const),
            pl.BlockSpec((1, CONV_DIM), const),
            pl.BlockSpec((1, LANES), const),
            pl.BlockSpec((1, LANES), const),
            pl.BlockSpec((1, D_INNER), const),
            pl.BlockSpec((1, D_INNER), const),
        ],
        out_specs=[
            pl.BlockSpec((sq, D_INNER), lambda s: (s, 0)),
            pl.BlockSpec((sq, D_INNER, D_STATE), lambda s: (s, 0, 0)),
            pl.BlockSpec((sq, CONV_W - 1, CONV_DIM), lambda s: (s, 0, 0)),
        ],
        out_shape=[
            jax.ShapeDtypeStruct((n, D_INNER), F32),
            jax.ShapeDtypeStruct((n, D_INNER, D_STATE), F32),
            jax.ShapeDtypeStruct((n, CONV_W - 1, CONV_DIM), F32),
        ],
        compiler_params=_params("arbitrary"),
        name="ssd_step",
    )(big, big, small, conv_state, ssm_state, lw["conv_w"], lw["conv_b"], lw["dt_bias"], lw["a_log"],
      lw["d_exp"], lw["ssm_norm_g"])


def _prep_kernel(sm_ref, cos_ref, sin_ref, qg_ref, kvg_ref, wqn_ref, wqr_ref, wkt_ref,
                 qlat_ref, qrope_ref, kc_ref, kr_ref, latf_ref, krf_ref, kct_ref, krt_ref, *, scale):
    sm = sm_ref[0]
    tm = sm.shape[0]
    half = QK_ROPE // 2
    qn = _rms(sm[:, SM_QA:SM_QA + Q_LORA], qg_ref[...]).astype(BF16)
    qnope = _dot(qn, wqn_ref[...])
    qr = _dot(qn, wqr_ref[...])
    cos = cos_ref[...]
    sin = sin_ref[...]
    wq = MLA_HEADS * QK_ROPE
    reps = wq // LANES
    cosq = jnp.concatenate([cos] * reps, axis=1)
    sinq = jnp.concatenate([sin] * reps, axis=1)
    lane_q = lax.broadcasted_iota(jnp.int32, (tm, wq), 1)
    swap = jnp.where((lane_q & (QK_ROPE - 1)) < half, pltpu.roll(qr, wq - half, 1), pltpu.roll(qr, half, 1))
    qrot = (qr * cosq + swap * sinq) * scale
    for h in range(MLA_HEADS):
        qrope_ref[0, h] = qrot[:, h * QK_ROPE:(h + 1) * QK_ROPE].astype(BF16)
        ql = _dot(qnope[:, h * QK_NOPE:(h + 1) * QK_NOPE].astype(BF16), wkt_ref[h]) * scale
        qlat_ref[0, h] = ql.astype(BF16)
    cn = _rms(sm[:, SM_LAT:SM_LAT + KV_LORA], kvg_ref[...])
    latf_ref[0] = cn
    kc_ref[0] = cn.astype(BF16)
    kct_ref[0] = cn.T.astype(BF16)
    tail = sm[:, SM_TAIL:SM_TAIL + LANES]
    lane_k = lax.broadcasted_iota(jnp.int32, (tm, LANES), 1)
    swapk = jnp.where(lane_k < half, pltpu.roll(tail, LANES - half, 1), pltpu.roll(tail, half, 1))
    krot = tail * cos + swapk * sin
    krf_ref[0] = krot[:, 0:QK_ROPE]
    kr_ref[0] = krot[:, 0:QK_ROPE].astype(BF16)
    krt_ref[0] = krot.T[0:QK_ROPE, :].astype(BF16)


def _mla_prep(small, cos, sin, lw, tm=256):
    nb, t, _ = small.shape
    tm = min(tm, t)
    scale = (QK_NOPE + QK_ROPE) ** -0.5 * math.log2(math.e)
    const2 = lambda b, i: (0, 0)
    return pl.pallas_call(
        functools.partial(_prep_kernel, scale=scale),
        grid=(nb, t // tm),
        in_specs=[
            pl.BlockSpec((1, tm, SM_W), lambda b, i: (b, i, 0)),
            pl.BlockSpec((tm, LANES), lambda b, i: (i, 0)),
            pl.BlockSpec((tm, LANES), lambda b, i: (i, 0)),
            pl.BlockSpec((1, Q_LORA), const2),
            pl.BlockSpec((1, KV_LORA), const2),
            pl.BlockSpec((Q_LORA, MLA_HEADS * QK_NOPE), const2),
            pl.BlockSpec((Q_LORA, MLA_HEADS * QK_ROPE), const2),
            pl.BlockSpec((MLA_HEADS, QK_NOPE, KV_LORA), lambda b, i: (0, 0, 0)),
        ],
        out_specs=[
            pl.BlockSpec((1, MLA_HEADS, tm, KV_LORA), lambda b, i: (b, 0, i, 0)),
            pl.BlockSpec((1, MLA_HEADS, tm, QK_ROPE), lambda b, i: (b, 0, i, 0)),
            pl.BlockSpec((1, tm, KV_LORA), lambda b, i: (b, i, 0)),
            pl.BlockSpec((1, tm, QK_ROPE), lambda b, i: (b, i, 0)),
            pl.BlockSpec((1, tm, KV_LORA), lambda b, i: (b, i, 0)),
            pl.BlockSpec((1, tm, QK_ROPE), lambda b, i: (b, i, 0)),
            pl.BlockSpec((1, KV_LORA, tm), lambda b, i: (b, 0, i)),
            pl.BlockSpec((1, QK_ROPE, tm), lambda b, i: (b, 0, i)),
        ],
        out_shape=[
            jax.ShapeDtypeStruct((nb, MLA_HEADS, t, KV_LORA), BF16),
            jax.ShapeDtypeStruct((nb, MLA_HEADS, t, QK_ROPE), BF16),
            jax.ShapeDtypeStruct((nb, t, KV_LORA), BF16),
            jax.ShapeDtypeStruct((nb, t, QK_ROPE), BF16),
            jax.ShapeDtypeStruct((nb, t, KV_LORA), F32),
            jax.ShapeDtypeStruct((nb, t, QK_ROPE), F32),
            jax.ShapeDtypeStruct((nb, KV_LORA, t), BF16),
            jax.ShapeDtypeStruct((nb, QK_ROPE, t), BF16),
        ],
        compiler_params=_params("arbitrary", "arbitrary"),
        name="mla_prep",
    )(small, cos, sin, lw["q_norm_g"], lw["kv_norm_g"], lw["w_q_nope"], lw["w_q_rope"], lw["w_k_t"])


def _lane_tiles(x):
    return [x[:, k * LANES:(k + 1) * LANES] for k in range(x.shape[1] // LANES)]


def _softmax_step(s, m_old=None, l_old=None):
    tiles = _lane_tiles(s)
    rows = s.shape[0]
    row_max = jnp.broadcast_to(jnp.max(functools.reduce(jnp.maximum, tiles), axis=1, keepdims=True), (rows, LANES))
    m_new = row_max if m_old is None else jnp.maximum(m_old, row_max)
    p_tiles = [jnp.exp2(t - m_new) for t in tiles]
    l_new = jnp.broadcast_to(jnp.sum(functools.reduce(jnp.add, p_tiles), axis=1, keepdims=True), (rows, LANES))
    alpha = None
    if m_old is not None:
        alpha = jnp.exp2(m_old - m_new)
        l_new = alpha * l_old + l_new
    return m_new, alpha, l_new, jnp.concatenate(p_tiles, axis=1)


def _attn_kernel(qlat_ref, qrope_ref, kc_ref, kct_ref, krt_ref, wv_ref, o_ref, m_ref, l_ref, acc_ref, *, tq, tk, hc):
    i = pl.program_id(1)
    cr = hc * tq
    reps = KV_LORA // LANES

    def block(j, diagonal):
        start = pl.multiple_of(j * tk, tk)
        kc = kc_ref[0, pl.ds(start, tk), :]
        kct = kct_ref[0, :, pl.ds(start, tk)]
        krt = krt_ref[0, :, pl.ds(start, tk)]
        if diagonal:
            qpos = i * tq + lax.broadcasted_iota(jnp.int32, (hc, tq, tk), 1).reshape(cr, tk)
            kpos = j * tk + lax.broadcasted_iota(jnp.int32, (cr, tk), 1)
            visible = kpos <= qpos
        for c in range(MLA_HEADS // hc):
            r0 = c * cr
            q1 = qlat_ref[0, c * hc:(c + 1) * hc].reshape(cr, KV_LORA)
            q2 = qrope_ref[0, c * hc:(c + 1) * hc].reshape(cr, QK_ROPE)
            s = _dot(q1, kct) + _dot(q2, krt)
            if diagonal:
                s = jnp.where(visible, s, NEG)
                m_old = l_old = None
            else:
                m_old = m_ref[r0:r0 + cr, :]
                l_old = l_ref[r0:r0 + cr, :]
            m_new, alpha, l_new, p = _softmax_step(s, m_old, l_old)
            l_ref[r0:r0 + cr, :] = l_new
            m_ref[r0:r0 + cr, :] = m_new
            pv = _dot(p.astype(BF16), kc)
            if diagonal:
                acc_ref[r0:r0 + cr, :] = pv
            else:
                acc_ref[r0:r0 + cr, :] = jnp.concatenate([alpha] * reps, axis=1) * acc_ref[r0:r0 + cr, :] + pv

    nfull = (i * tq) // tk
    block(nfull, True)

    def body(j, carry):
        block(j, False)
        return carry

    lax.fori_loop(0, nfull, body, 0)
    for h in range(MLA_HEADS):
        l_h = jnp.concatenate([l_ref[h * tq:(h + 1) * tq, :]] * reps, axis=1)
        o_lat = (acc_ref[h * tq:(h + 1) * tq, :] / l_h).astype(BF16)
        o_ref[0, :, h * V_DIM:(h + 1) * V_DIM] = _dot(o_lat, wv_ref[h]).astype(o_ref.dtype)


def _attn_prompt(qlat, qrope, kc, kct, krt, wv, tq=128, tk=256, hc=16):
    nb, _, t, _ = qlat.shape
    tk = min(tk, t)
    tq = min(tq, tk)
    rows = MLA_HEADS * tq
    return pl.pallas_call(
        functools.partial(_attn_kernel, tq=tq, tk=tk, hc=hc),
        grid=(nb, t // tq),
        in_specs=[
            pl.BlockSpec((1, MLA_HEADS, tq, KV_LORA), lambda b, i: (b, 0, i, 0)),
            pl.BlockSpec((1, MLA_HEADS, tq, QK_ROPE), lambda b, i: (b, 0, i, 0)),
            pl.BlockSpec((1, t, KV_LORA), lambda b, i: (b, 0, 0)),
            pl.BlockSpec((1, KV_LORA, t), lambda b, i: (b, 0, 0)),
            pl.BlockSpec((1, QK_ROPE, t), lambda b, i: (b, 0, 0)),
            pl.BlockSpec((MLA_HEADS, KV_LORA, V_DIM), lambda b, i: (0, 0, 0)),
        ],
        out_specs=pl.BlockSpec((1, tq, MLA_HEADS * V_DIM), lambda b, i: (b, i, 0)),
        out_shape=jax.ShapeDtypeStruct((nb, t, MLA_HEADS * V_DIM), BF16),
        scratch_shapes=[pltpu.VMEM((rows, LANES), F32), pltpu.VMEM((rows, LANES), F32),
                        pltpu.VMEM((rows, KV_LORA), F32)],
        compiler_params=_params("arbitrary", "arbitrary"),
        name="attn_prompt",
    )(qlat, qrope, kc, kct, krt, wv)


def _decode_kernel(pt_ref, qlat_ref, qrope_ref, kcn_ref, krn_ref, lat_hbm, krt_hbm, o_ref,
                   lat_buf, krt_buf, s_buf, sem, *, layer, npages, chunk_pages):
    s = pl.program_id(0)
    n = pl.num_programs(0)
    slot = lax.rem(s, 2)
    ck = chunk_pages * PAGE_SIZE
    nchunk = npages // chunk_pages

    def page_copies(seq, slot_, p):
        pg = pt_ref[seq * npages + p]
        keys = pl.ds(p * PAGE_SIZE, PAGE_SIZE)
        return (pltpu.make_async_copy(lat_hbm.at[layer, pg], lat_buf.at[slot_, keys, :], sem.at[0, slot_]),
                pltpu.make_async_copy(krt_hbm.at[layer, pg], krt_buf.at[slot_, :, keys], sem.at[1, slot_]))

    def start_all(seq, slot_):
        for p in range(npages):
            for cp in page_copies(seq, slot_, p):
                cp.start()

    @pl.when(s == 0)
    def _():
        start_all(0, 0)

    @pl.when(s + 1 < n)
    def _():
        start_all(s + 1, 1 - slot)

    for p in range(npages):
        for cp in page_copies(s, slot, p):
            cp.wait()

    q1 = qlat_ref[0]
    q2 = qrope_ref[0]
    kn = kcn_ref[0].astype(F32)
    s_new = (jnp.sum(q1.astype(F32) * kn, axis=1, keepdims=True)
             + jnp.sum(q2.astype(F32) * krn_ref[0].astype(F32), axis=1, keepdims=True))

    def chunk_keys(c):
        lat = lat_buf[slot, c * ck:(c + 1) * ck, :].astype(BF16)
        krt = krt_buf[slot, :, c * ck:(c + 1) * ck].astype(BF16)
        return lat, krt

    mx = jnp.broadcast_to(s_new, (MLA_HEADS, LANES))
    for c in range(nchunk):
        lat, krt = chunk_keys(c)
        sc = _dot_nt(q1, lat) + _dot(q2, krt)
        s_buf[:, c * ck:(c + 1) * ck] = sc
        mx = jnp.maximum(mx, functools.reduce(jnp.maximum, _lane_tiles(sc)))
    m = jnp.max(mx, axis=1, keepdims=True)
    p_new = jnp.exp2(s_new - m)
    l = p_new
    acc = p_new * kn
    for c in range(nchunk):
        lat, _ = chunk_keys(c)
        p = jnp.exp2(s_buf[:, c * ck:(c + 1) * ck] - m)
        l = l + jnp.sum(p, axis=1, keepdims=True)
        acc = acc + _dot(p.astype(BF16), lat)
    o_ref[0] = (acc / l).astype(o_ref.dtype)


def _attn_decode(layer, page_table, cache_latent, cache_krope_t, qlat, qrope, kcn, krn, chunk_pages=8):
    n, npages = page_table.shape
    chunk_pages = math.gcd(chunk_pages, npages)
    n_keys = npages * PAGE_SIZE
    pt = page_table.reshape(-1)
    row = lambda s, pt_ref: (s, 0, 0)
    grid_spec = pltpu.PrefetchScalarGridSpec(
        num_scalar_prefetch=1,
        grid=(n,),
        in_specs=[
            pl.BlockSpec((1, MLA_HEADS, KV_LORA), row),
            pl.BlockSpec((1, MLA_HEADS, QK_ROPE), row),
            pl.BlockSpec((1, 1, KV_LORA), row),
            pl.BlockSpec((1, 1, QK_ROPE), row),
            pl.BlockSpec(memory_space=pl.ANY),
            pl.BlockSpec(memory_space=pl.ANY),
        ],
        out_specs=pl.BlockSpec((1, MLA_HEADS, KV_LORA), row),
        scratch_shapes=[
            pltpu.VMEM((2, n_keys, KV_LORA), F32),
            pltpu.VMEM((2, QK_ROPE, n_keys), F32),
            pltpu.VMEM((MLA_HEADS, n_keys), F32),
            pltpu.SemaphoreType.DMA((2, 2)),
        ],
    )
    return pl.pallas_call(
        functools.partial(_decode_kernel, layer=layer, npages=npages, chunk_pages=chunk_pages),
        grid_spec=grid_spec,
        out_shape=jax.ShapeDtypeStruct((n, MLA_HEADS, KV_LORA), BF16),
        compiler_params=_params("arbitrary"),
        name="attn_decode",
    )(pt, qlat, qrope, kcn, krn, cache_latent, cache_krope_t)


def _vproj_kernel(o_ref, wv_ref, y_ref):
    for h in range(MLA_HEADS):
        y_ref[:, h * V_DIM:(h + 1) * V_DIM] = _dot(o_ref[h], wv_ref[h]).astype(y_ref.dtype)


def _vproj(o_lat, wv):
    n = o_lat.shape[1]
    return pl.pallas_call(
        _vproj_kernel,
        out_shape=jax.ShapeDtypeStruct((n, MLA_HEADS * V_DIM), BF16),
        name="v_proj",
    )(o_lat, wv)


def _merge_kernel(ys_ref, ym_ref, g_ref, x_ref, wbs_ref, wbm_ref, wo_ref, lg_ref, lb_ref, o_ref):
    gates = g_ref[...].astype(F32)
    a = _dot(ys_ref[...].astype(BF16), wbs_ref[...])
    b = _dot(ym_ref[...], wbm_ref[...])
    m = jax.nn.sigmoid(gates[:, 0:D_MODEL]) * a + jax.nn.sigmoid(gates[:, D_MODEL:2 * D_MODEL]) * b
    r = ALPHA * x_ref[...] + _dot(m.astype(BF16), wo_ref[...])
    o_ref[...] = _layernorm(r, lg_ref[...], lb_ref[...])


def _merge(y_ssm, y_mla, big, x, lw, tm=512):
    m = x.shape[0]
    tm = min(tm, m)
    const = lambda i: (0, 0)
    return pl.pallas_call(
        _merge_kernel,
        grid=(m // tm,),
        in_specs=[
            pl.BlockSpec((tm, D_INNER), lambda i: (i, 0)),
            pl.BlockSpec((tm, MLA_HEADS * V_DIM), lambda i: (i, 0)),
            pl.BlockSpec((tm, 2 * D_MODEL), lambda i: (i, 3)),
            pl.BlockSpec((tm, D_MODEL), lambda i: (i, 0)),
            pl.BlockSpec((D_INNER, D_MODEL), const),
            pl.BlockSpec((MLA_HEADS * V_DIM, D_MODEL), const),
            pl.BlockSpec((D_MODEL, D_MODEL), const),
            pl.BlockSpec((1, D_MODEL), const),
            pl.BlockSpec((1, D_MODEL), const),
        ],
        out_specs=pl.BlockSpec((tm, D_MODEL), lambda i: (i, 0)),
        out_shape=jax.ShapeDtypeStruct((m, D_MODEL), F32),
        compiler_params=_params("arbitrary"),
        name="merge_ln",
    )(y_ssm, y_mla, big, x, lw["w_br_ssm"], lw["w_br_mla"], lw["w_out"], lw["ln1_g"], lw["ln1_b"])


def _swiglu_hidden(xb, wgu):
    tf = wgu.shape[1] // 2
    gu = _dot(xb, wgu)
    return (_silu(gu[:, 0:tf]) * gu[:, tf:2 * tf]).astype(BF16)


def _fuse_gate_up(wg, wu, tf):
    pieces = []
    for j in range(wg.shape[-1] // tf):
        pieces += [wg[..., j * tf:(j + 1) * tf], wu[..., j * tf:(j + 1) * tf]]
    return jnp.concatenate(pieces, axis=-1)


def _ffn_kernel(x_ref, wgu_ref, wd_ref, lg_ref, lb_ref, o_ref, acc_ref):
    j = pl.program_id(1)
    h = _swiglu_hidden(x_ref[...].astype(BF16), wgu_ref[...])
    part = _dot(h, wd_ref[...])

    @pl.when(j == 0)
    def _():
        acc_ref[...] = part

    @pl.when(j > 0)
    def _():
        acc_ref[...] += part

    @pl.when(j == pl.num_programs(1) - 1)
    def _():
        o_ref[...] = _layernorm(ALPHA * x_ref[...] + acc_ref[...], lg_ref[...], lb_ref[...])


def _ff_tile(d_ff):
    half = d_ff // 2
    return half if half % LANES == 0 else d_ff


def _ffn(x, wgu, wd, ln_g, ln_b, tm=512):
    m = x.shape[0]
    d_ff = wd.shape[0]
    tm = min(tm, m)
    tf = _ff_tile(d_ff)
    const = lambda i, j: (0, 0)
    return pl.pallas_call(
        _ffn_kernel,
        grid=(m // tm, d_ff // tf),
        in_specs=[
            pl.BlockSpec((tm, D_MODEL), lambda i, j: (i, 0)),
            pl.BlockSpec((D_MODEL, 2 * tf), lambda i, j: (0, j)),
            pl.BlockSpec((tf, D_MODEL), lambda i, j: (j, 0)),
            pl.BlockSpec((1, D_MODEL), const),
            pl.BlockSpec((1, D_MODEL), const),
        ],
        out_specs=pl.BlockSpec((tm, D_MODEL), lambda i, j: (i, 0)),
        out_shape=jax.ShapeDtypeStruct((m, D_MODEL), F32),
        scratch_shapes=[pltpu.VMEM((tm, D_MODEL), F32)],
        compiler_params=_params("arbitrary", "arbitrary"),
        name="ffn_ln",
    )(x, wgu, wd, ln_g, ln_b)


def _router_kernel(x_ref, rw_ref, c_ref):
    logits = jnp.dot(x_ref[...], rw_ref[...], precision=lax.Precision.HIGHEST, preferred_element_type=F32)
    lane = lax.broadcasted_iota(jnp.int32, logits.shape, 1)
    logits = jnp.where(lane < N_EXPERTS, logits, NEG)
    m1 = jnp.max(logits, axis=1, keepdims=True)
    i1 = jnp.min(jnp.where(logits == m1, lane, LANES), axis=1, keepdims=True)
    rest = jnp.where(lane == i1, NEG, logits)
    m2 = jnp.max(rest, axis=1, keepdims=True)
    i2 = jnp.min(jnp.where(rest == m2, lane, LANES), axis=1, keepdims=True)
    e2 = jnp.exp(m2 - m1)
    den = 1.0 + e2
    c_ref[...] = jnp.where(lane == i1, 1.0 / den, 0.0) + jnp.where(lane == i2, e2 / den, 0.0)


def _router(x, rw_pad, tm=512):
    m = x.shape[0]
    tm = min(tm, m)
    return pl.pallas_call(
        _router_kernel,
        grid=(m // tm,),
        in_specs=[pl.BlockSpec((tm, D_MODEL), lambda i: (i, 0)),
                  pl.BlockSpec((D_MODEL, LANES), lambda i: (0, 0))],
        out_specs=pl.BlockSpec((tm, LANES), lambda i: (i, 0)),
        out_shape=jax.ShapeDtypeStruct((m, LANES), F32),
        compiler_params=_params("arbitrary"),
        name="router",
    )(x, rw_pad)


def _moe_kernel(x_ref, c_ref, wgu_ref, wd_ref, lg_ref, lb_ref, o_ref, acc_ref):
    e = pl.program_id(1)
    j = pl.program_id(2)
    h = _swiglu_hidden(x_ref[...].astype(BF16), wgu_ref[0])
    comb = c_ref[...]
    lane = lax.broadcasted_iota(jnp.int32, comb.shape, 1)
    w_e = jnp.sum(jnp.where(lane == e, comb, 0.0), axis=1, keepdims=True)
    part = w_e * _dot(h, wd_ref[0])
    first = jnp.logical_and(e == 0, j == 0)

    @pl.when(first)
    def _():
        acc_ref[...] = part

    @pl.when(jnp.logical_not(first))
    def _():
        acc_ref[...] += part

    @pl.when(jnp.logical_and(e == pl.num_programs(1) - 1, j == pl.num_programs(2) - 1))
    def _():
        o_ref[...] = _layernorm(ALPHA * x_ref[...] + acc_ref[...], lg_ref[...], lb_ref[...])


def _moe(x, comb, wgu, wd, ln_g, ln_b, tm=512):
    m = x.shape[0]
    n_e, d_ff, _ = wd.shape
    tm = min(tm, m)
    tf = _ff_tile(d_ff)
    const = lambda i, e, j: (0, 0)
    return pl.pallas_call(
        _moe_kernel,
        grid=(m // tm, n_e, d_ff // tf),
        in_specs=[
            pl.BlockSpec((tm, D_MODEL), lambda i, e, j: (i, 0)),
            pl.BlockSpec((tm, LANES), lambda i, e, j: (i, 0)),
            pl.BlockSpec((1, D_MODEL, 2 * tf), lambda i, e, j: (e, 0, j)),
            pl.BlockSpec((1, tf, D_MODEL), lambda i, e, j: (e, j, 0)),
            pl.BlockSpec((1, D_MODEL), const),
            pl.BlockSpec((1, D_MODEL), const),
        ],
        out_specs=pl.BlockSpec((tm, D_MODEL), lambda i, e, j: (i, 0)),
        out_shape=jax.ShapeDtypeStruct((m, D_MODEL), F32),
        scratch_shapes=[pltpu.VMEM((tm, D_MODEL), F32)],
        compiler_params=_params("arbitrary", "arbitrary", "arbitrary"),
        name="moe_ln",
    )(x, comb, wgu, wd, ln_g, ln_b)


def _layer_weights(l, w_in, conv_w, conv_b, dt_bias, a_log, d_skip, ssm_norm_g, q_norm_g, w_qb, kv_norm_g,
                   w_kvb, w_br_ssm, w_br_mla, w_out, ln1_g, ln1_b, ln2_g, ln2_b):
    wi = w_in[l]
    o_xbc = D_INNER
    o_dt = o_xbc + CONV_DIM
    o_qa = o_dt + SSM_HEADS
    o_kv = o_qa + Q_LORA
    o_gs = o_kv + KV_LORA + QK_ROPE
    o_gm = o_gs + D_MODEL
    w_big = jnp.concatenate([wi[:, o_xbc:o_dt], wi[:, 0:o_xbc], wi[:, o_gs:o_gm], wi[:, o_gm:]], axis=1).astype(BF16)
    w_small = jnp.concatenate(
        [wi[:, o_qa:o_kv], wi[:, o_kv:o_gs], wi[:, o_dt:o_qa],
         jnp.zeros((D_MODEL, LANES - QK_ROPE - SSM_HEADS), F32)], axis=1).astype(BF16)

    def tail_pad(v):
        return jnp.zeros((1, LANES), F32).at[0, DT_OFF:DT_OFF + SSM_HEADS].set(v)

    wq = w_qb[l].reshape(Q_LORA, MLA_HEADS, QK_NOPE + QK_ROPE)
    wkv = w_kvb[l].reshape(KV_LORA, MLA_HEADS, QK_NOPE + V_DIM)
    return {
        "w_big": w_big,
        "w_small": w_small,
        "conv_w": conv_w[l],
        "conv_b": conv_b[l].reshape(1, CONV_DIM),
        "dt_bias": tail_pad(dt_bias[l]),
        "a_log": tail_pad(a_log[l]),
        "d_exp": jnp.repeat(d_skip[l], SSM_HEADDIM).reshape(1, D_INNER),
        "ssm_norm_g": ssm_norm_g[l].reshape(1, D_INNER),
        "q_norm_g": q_norm_g[l].reshape(1, Q_LORA),
        "kv_norm_g": kv_norm_g[l].reshape(1, KV_LORA),
        "w_q_nope": wq[:, :, :QK_NOPE].reshape(Q_LORA, MLA_HEADS * QK_NOPE).astype(BF16),
        "w_q_rope": wq[:, :, QK_NOPE:].reshape(Q_LORA, MLA_HEADS * QK_ROPE).astype(BF16),
        "w_k_t": jnp.transpose(wkv[:, :, :QK_NOPE], (1, 2, 0)).astype(BF16),
        "w_v": jnp.transpose(wkv[:, :, QK_NOPE:], (1, 0, 2)).astype(BF16),
        "w_br_ssm": w_br_ssm[l].astype(BF16),
        "w_br_mla": w_br_mla[l].astype(BF16),
        "w_out": w_out[l].astype(BF16),
        "ln1_g": ln1_g[l].reshape(1, D_MODEL),
        "ln1_b": ln1_b[l].reshape(1, D_MODEL),
        "ln2_g": ln2_g[l].reshape(1, D_MODEL),
        "ln2_b": ln2_b[l].reshape(1, D_MODEL),
    }


def _rope_tables(pos):
    half = QK_ROPE // 2
    inv = ROPE_BASE ** (-jnp.arange(0, QK_ROPE, 2, dtype=F32) / QK_ROPE)
    ang = pos.astype(F32)[:, None] * inv[None, :]
    cos, sin = jnp.cos(ang), jnp.sin(ang)
    cos_t = jnp.tile(cos, (1, LANES // half))
    sin_t = jnp.tile(jnp.concatenate([-sin, sin], axis=1), (1, LANES // QK_ROPE))
    return cos_t, sin_t


def kernel(x_prompt, x_sample, cache_latent, cache_krope, state_ssm, state_conv, page_table, w_in, conv_w, conv_b, dt_bias, a_log, d_skip, ssm_norm_g, q_norm_g, w_qb, kv_norm_g, w_kvb, w_br_ssm, w_br_mla, w_out, ln1_g, ln1_b, ln2_g, ln2_b, ffn_w_gate, ffn_w_up, ffn_w_down, router_w, moe_w_gate, moe_w_up, moe_w_down):
    nb, t, _ = x_prompt.shape
    ns = x_sample.shape[0]
    assert x_sample.shape[1] == 1
    n_past = page_table.shape[1] * PAGE_SIZE
    cos_p, sin_p = _rope_tables(jnp.arange(t, dtype=jnp.int32))
    cos_s, sin_s = _rope_tables(jnp.full((ns,), n_past, dtype=jnp.int32))

    cache_krope_t = jnp.swapaxes(cache_krope, 2, 3)
    state_ssm_r = state_ssm.reshape(DEPTH, ns, D_INNER, D_STATE)

    xp = x_prompt.reshape(nb * t, D_MODEL)
    xs = x_sample.reshape(ns, D_MODEL)
    outs = [[] for _ in range(8)]
    for l in range(DEPTH):
        lw = _layer_weights(l, w_in, conv_w, conv_b, dt_bias, a_log, d_skip, ssm_norm_g, q_norm_g, w_qb,
                            kv_norm_g, w_kvb, w_br_ssm, w_br_mla, w_out, ln1_g, ln1_b, ln2_g, ln2_b)
        big_p = _matmul(xp, lw["w_big"], BF16)
        small_p = _matmul(xp, lw["w_small"], F32)
        big_p3 = big_p.reshape(nb, t, -1)
        small_p3 = small_p.reshape(nb, t, SM_W)
        y_ssm_p, h_p, cv_p = _ssd_prompt(big_p3, small_p3, lw)
        qlat, qrope, kc, _, lat_f, kr_f, kct, krt = _mla_prep(small_p3, cos_p, sin_p, lw)
        y_mla_p = _attn_prompt(qlat, qrope, kc, kct, krt, lw["w_v"])
        xp = _merge(y_ssm_p.reshape(nb * t, D_INNER), y_mla_p.reshape(nb * t, -1), big_p, xp, lw)
        big_s = _matmul(xs, lw["w_big"], F32)
        small_s = _matmul(xs, lw["w_small"], F32)
        y_ssm_s, h_s, cv_s = _ssd_step(l, big_s, small_s, state_conv[l], state_ssm_r, lw)
        qlat_s, qrope_s, kc_s, kr_s, lat_fs, kr_fs, _, _ = _mla_prep(small_s.reshape(1, ns, SM_W), cos_s, sin_s, lw)
        o_lat = _attn_decode(l, page_table, cache_latent, cache_krope_t,
                             jnp.transpose(qlat_s[0], (1, 0, 2)), jnp.transpose(qrope_s[0], (1, 0, 2)),
                             kc_s.reshape(ns, 1, KV_LORA), kr_s.reshape(ns, 1, QK_ROPE))
        y_mla_s = _vproj(jnp.transpose(o_lat, (1, 0, 2)), lw["w_v"])
        xs = _merge(y_ssm_s.reshape(ns, D_INNER), y_mla_s, big_s, xs, lw)
        i = l // 2
        if l % 2 == 0:
            tf = _ff_tile(ffn_w_gate.shape[-1])
            wgu = _fuse_gate_up(ffn_w_gate[i].astype(BF16), ffn_w_up[i].astype(BF16), tf)
            wd = ffn_w_down[i].astype(BF16)
            xp = _ffn(xp, wgu, wd, lw["ln2_g"], lw["ln2_b"])
            xs = _ffn(xs, wgu, wd, lw["ln2_g"], lw["ln2_b"])
        else:
            tf = _ff_tile(moe_w_gate.shape[-1])
            wgu = _fuse_gate_up(moe_w_gate[i].astype(BF16), moe_w_up[i].astype(BF16), tf)
            wd = moe_w_down[i].astype(BF16)
            rw = jnp.zeros((D_MODEL, LANES), F32).at[:, :N_EXPERTS].set(router_w[i])
            xp = _moe(xp, _router(xp, rw), wgu, wd, lw["ln2_g"], lw["ln2_b"])
            xs = _moe(xs, _router(xs, rw), wgu, wd, lw["ln2_g"], lw["ln2_b"])
        for k, v in enumerate((lat_f, kr_f, h_p.reshape(nb, SSM_HEADS, SSM_HEADDIM, D_STATE), cv_p,
                               lat_fs.reshape(ns, 1, KV_LORA), kr_fs.reshape(ns, 1, QK_ROPE),
                               h_s.reshape(ns, SSM_HEADS, SSM_HEADDIM, D_STATE), cv_s)):
            outs[k].append(v)

    return (xp.reshape(nb, t, D_MODEL), xs.reshape(ns, 1, D_MODEL)) + tuple(jnp.stack(o) for o in outs)
```
